```python
import jax, jax.numpy as jnp
from jax import lax
import numpy as np

D_MODEL = 1024
BATCH = 8
SEQ = 2048
DEPTH = 1

CHUNK = 64
D_SHORT = D_MODEL
SHORT_CONV = 3
SSM_EXPAND = 2
D_INNER = SSM_EXPAND * D_MODEL
SSM_HEAD_DIM = 64
SSM_HEADS = D_INNER // SSM_HEAD_DIM
SSM_GROUPS = 8
D_STATE = 128
SSM_CONV = 4
D_XBC = D_INNER + 2 * SSM_GROUPS * D_STATE
D_FF = 2816
NORM_EPS = 1e-5

kernel_name = "hybrid_shortconv_ssd_macaron_block"


def rms_norm(x, w):
    xf = x.astype(jnp.float32)
    y = xf * lax.rsqrt(jnp.mean(xf * xf, axis=-1, keepdims=True) + NORM_EPS)
    return (y * w.astype(jnp.float32)).astype(x.dtype)


def gated_group_rms_norm(y, z, w, groups):
    yf = y.astype(jnp.float32) * jax.nn.silu(z.astype(jnp.float32))
    shp = yf.shape
    yg = yf.reshape(shp[:-1] + (groups, shp[-1] // groups))
    yg = yg * lax.rsqrt(jnp.mean(yg * yg, axis=-1, keepdims=True) + NORM_EPS)
    return (yg.reshape(shp) * w.astype(jnp.float32)).astype(y.dtype)


def swiglu(h, w_in, w_out):
    g, u = jnp.split(h @ w_in, 2, axis=-1)
    return (jax.nn.silu(g) * u) @ w_out


def causal_dwconv(x, w):
    k = w.shape[0]
    return lax.conv_general_dilated(
        x, w[:, None, :].astype(x.dtype), window_strides=(1,),
        padding=[(k - 1, 0)], dimension_numbers=("NWC", "WIO", "NWC"),
        feature_group_count=x.shape[-1])


def segsum(a):
    t = a.shape[-1]
    ae = jnp.broadcast_to(a[..., None], a.shape + (t,))
    ae = jnp.where(jnp.tril(jnp.ones((t, t), bool), -1), ae, 0.0)
    ss = jnp.cumsum(ae, axis=-2)
    return jnp.where(jnp.tril(jnp.ones((t, t), bool), 0), ss, -jnp.inf)


def ssd_scan(x, dt, a_coef, bm, cm):
    b, s, h, p = x.shape
    g, n = bm.shape[-2:]
    e = h // g
    c = s // CHUNK
    xdt = (x * dt[..., None]).reshape(b, c, CHUNK, g, e, p)
    bc = bm.reshape(b, c, CHUNK, g, n)
    cc = cm.reshape(b, c, CHUNK, g, n)
    a = (dt * a_coef).reshape(b, c, CHUNK, g, e).transpose(0, 3, 4, 1, 2)
    a_cs = jnp.cumsum(a, axis=-1)
    decay = jnp.exp(segsum(a))
    cb = jnp.einsum("bclgn,bcsgn->bgcls", cc, bc)
    y_diag = jnp.einsum("bgecls,bcsgep->bclgep", cb[:, :, None] * decay, xdt)
    decay_states = jnp.exp(a_cs[..., -1:] - a_cs)
    states = jnp.einsum("bclgn,bgecl,bclgep->bcgepn", bc, decay_states, xdt)
    states = jnp.concatenate([jnp.zeros_like(states[:, :1]), states], axis=1)
    chunk_decay = jnp.exp(segsum(jnp.pad(a_cs[..., -1], ((0, 0), (0, 0), (0, 0), (1, 0)))))
    states = jnp.einsum("bgezc,bcgepn->bzgepn", chunk_decay, states)[:, :-1]
    y_off = jnp.einsum("bclgn,bcgepn,bgecl->bclgep", cc, states, jnp.exp(a_cs))
    return (y_diag + y_off).reshape(b, s, h, p)


def setup_inputs(seed: int = 0) -> dict:
    key = jax.random.key(seed)
    ks = jax.random.split(key, 24)
    L, D = DEPTH, D_MODEL
    n_in = 3 * D_SHORT + D_INNER + D_XBC + SSM_HEADS + 2 * D_MODEL

    def nrm(k, shape, fan_in):
        return jax.random.normal(k, shape, jnp.float32) * fan_in ** -0.5

    def gain(k, shape):
        return 1.0 + 0.02 * jax.random.normal(k, shape, jnp.float32)

    dt0 = jnp.exp(jax.random.uniform(ks[10], (L, SSM_HEADS), jnp.float32,
                                     np.log(1e-3), np.log(1e-1)))
    dt_bias = dt0 + jnp.log(-jnp.expm1(-dt0))
    a_log = jnp.log(jax.random.uniform(ks[11], (L, SSM_HEADS), jnp.float32, 1.0, 16.0))
    return {
        "x": jax.random.normal(ks[0], (BATCH, SEQ, D), jnp.float32),
        "ffn1_norm": gain(ks[1], (L, D)),
        "ffn1_w_in": nrm(ks[2], (L, D, 2 * D_FF), D),
        "ffn1_w_out": nrm(ks[3], (L, D_FF, D), D_FF),
        "mix_norm": gain(ks[4], (L, D)),
        "w_in": nrm(ks[5], (L, D, n_in), D),
        "short_conv_w": nrm(ks[6], (L, SHORT_CONV, D_SHORT), SHORT_CONV),
        "short_w_out": nrm(ks[7], (L, D_SHORT, D), D_SHORT),
        "ssm_conv_w": nrm(ks[8], (L, SSM_CONV, D_XBC), SSM_CONV),
        "ssm_conv_b": 0.02 * jax.random.normal(ks[9], (L, D_XBC), jnp.float32),
        "ssm_dt_bias": dt_bias,
        "ssm_A_log": a_log,
        "ssm_D": gain(ks[12], (L, SSM_HEADS)),
        "ssm_norm": gain(ks[13], (L, D_INNER)),
        "ssm_w_out": nrm(ks[14], (L, D_INNER, D), D_INNER),
        "w_out": nrm(ks[15], (L, D, D), D),
        "ffn2_norm": gain(ks[16], (L, D)),
        "ffn2_w_in": nrm(ks[17], (L, D, 2 * D_FF), D),
        "ffn2_w_out": nrm(ks[18], (L, D_FF, D), D_FF),
        "final_norm": gain(ks[19], (D,)),
    }


def reference(x, ffn1_norm, ffn1_w_in, ffn1_w_out, mix_norm, w_in, short_conv_w,
              short_w_out, ssm_conv_w, ssm_conv_b, ssm_dt_bias, ssm_A_log, ssm_D,
              ssm_norm, ssm_w_out, w_out, ffn2_norm, ffn2_w_in, ffn2_w_out,
              final_norm):
    b, s, _ = x.shape
    sizes = [D_SHORT, D_SHORT, D_SHORT, D_INNER, D_XBC, SSM_HEADS, D_MODEL, D_MODEL]
    cuts = [int(v) for v in np.cumsum(sizes)[:-1]]
    for l in range(DEPTH):
        x = x + 0.5 * swiglu(rms_norm(x, ffn1_norm[l]), ffn1_w_in[l], ffn1_w_out[l])

        h = rms_norm(x, mix_norm[l])
        b_gate, c_gate, xa, z, xbc, dt_raw, ga, gb = jnp.split(h @ w_in[l], cuts, axis=-1)

        va = causal_dwconv(c_gate * xa, short_conv_w[l])
        y_a = (b_gate * va) @ short_w_out[l]

        xbc = jax.nn.silu(causal_dwconv(xbc, ssm_conv_w[l]) + ssm_conv_b[l])
        xs, bm, cm = jnp.split(xbc, [D_INNER, D_INNER + SSM_GROUPS * D_STATE], axis=-1)
        xs = xs.reshape(b, s, SSM_HEADS, SSM_HEAD_DIM).astype(jnp.float32)
        bm = bm.reshape(b, s, SSM_GROUPS, D_STATE).astype(jnp.float32)
        cm = cm.reshape(b, s, SSM_GROUPS, D_STATE).astype(jnp.float32)
        dt = jax.nn.softplus(dt_raw.astype(jnp.float32) + ssm_dt_bias[l].astype(jnp.float32))
        a_coef = -jnp.exp(ssm_A_log[l].astype(jnp.float32))
        y_ssm = ssd_scan(xs, dt, a_coef, bm, cm) + xs * ssm_D[l].astype(jnp.float32)[:, None]
        y_ssm = y_ssm.reshape(b, s, D_INNER).astype(x.dtype)
        y_b = gated_group_rms_norm(y_ssm, z, ssm_norm[l], SSM_GROUPS) @ ssm_w_out[l]

        merged = jax.nn.sigmoid(ga) * y_a + jax.nn.sigmoid(gb) * y_b
        x = x + merged @ w_out[l]

        x = x + 0.5 * swiglu(rms_norm(x, ffn2_norm[l]), ffn2_w_in[l], ffn2_w_out[l])
    return rms_norm(x, final_norm)
```

```python
import functools

import jax
import jax.numpy as jnp
import numpy as np
from jax import lax
from jax.experimental import pallas as pl
from jax.experimental.pallas import tpu as pltpu

D_MODEL = 1024
D_FF = 2816
D_SHORT = D_MODEL
SHORT_CONV = 3
D_INNER = 2048
SSM_HEADS = 32
SSM_HEAD_DIM = 64
SSM_GROUPS = 8
D_STATE = 128
SSM_CONV = 4
D_XBC = D_INNER + 2 * SSM_GROUPS * D_STATE
CHUNK = 64
NORM_EPS = 1e-5

HEADS_PER_GROUP = SSM_HEADS // SSM_GROUPS
GROUP_WIDTH = HEADS_PER_GROUP * SSM_HEAD_DIM
LANES = 128
SUBLANES = 8
HEAD_COPIES = LANES // SSM_HEADS

F32 = jnp.float32
BF16 = jnp.bfloat16

VMEM_LIMIT_BYTES = 58 * 1024 * 1024
MASKED = -1e30


def _rms_norm(x, gain):
    ms = jnp.mean(x * x, axis=-1, keepdims=True)
    return x * lax.rsqrt(ms + NORM_EPS) * gain


def _sigmoid(x):
    return 1.0 / (1.0 + jnp.exp(-x))


def _dot(a, b):
    return jnp.dot(a, b, preferred_element_type=F32)


def _split2(x):
    hi = x.astype(BF16).astype(F32)
    lo = (x - hi).astype(BF16).astype(F32)
    return hi, lo


def _resident(shape):
    nd = len(shape)
    return pl.BlockSpec(shape, lambda *_: (0,) * nd, pipeline_mode=pl.Buffered(1))


def _ffn_body(x_ref, gain_ref, w_in_ref, w_out_ref, fin_ref, o_ref, *, ff_chunk, final_norm):
    x = x_ref[...]
    h = _rms_norm(x, gain_ref[...]).astype(BF16)
    acc = jnp.zeros(x.shape, F32)
    for j in range(D_FF // ff_chunk):
        lo = j * ff_chunk
        g = _dot(h, w_in_ref[:, lo:lo + ff_chunk])
        u = _dot(h, w_in_ref[:, D_FF + lo:D_FF + lo + ff_chunk])
        a = (g * _sigmoid(g) * u).astype(BF16)
        acc = acc + _dot(a, w_out_ref[lo:lo + ff_chunk, :])
    y = x + 0.5 * acc
    if final_norm:
        y = _rms_norm(y, fin_ref[...])
    o_ref[...] = y


def _ffn(x2d, gain, w_in, w_out, fin, *, final_norm, tm=512, ff_chunk=256):
    t, d = x2d.shape
    body = functools.partial(_ffn_body, ff_chunk=ff_chunk, final_norm=final_norm)
    return pl.pallas_call(
        body,
        grid=(t // tm,),
        in_specs=[
            pl.BlockSpec((tm, d), lambda i: (i, 0)),
            _resident((1, d)),
            _resident(w_in.shape),
            _resident(w_out.shape),
            _resident((1, d)),
        ],
        out_specs=pl.BlockSpec((tm, d), lambda i: (i, 0)),
        out_shape=jax.ShapeDtypeStruct((t, d), F32),
        compiler_params=pltpu.CompilerParams(
            dimension_semantics=("arbitrary",), vmem_limit_bytes=VMEM_LIMIT_BYTES),
        name="ffn_final" if final_norm else "ffn",
    )(x2d, gain, w_in, w_out, fin)


def _mixer_body(x_ref, gain_ref, wa_ref, wz_ref, wxbc_ref, wdt_ref, wg_ref,
                cwa_ref, swo_ref, cwb_ref, cbb_ref, dtb_ref, alog_ref, dexp_ref,
                snorm_ref, sswo_ref, wo_ref, tril_ref, expand_ref, bdmask_ref,
                o_ref,
                ubuf, xbuf, st_ref, xs_ref, cscol_ref, xdt_ref, xpr_ref, cm_ref, bmtd_ref, cst_ref,
                *, tm):
    n_chunks = tm // CHUNK

    @pl.when(pl.program_id(1) == 0)
    def _():
        ubuf[0:SUBLANES, :] = jnp.zeros((SUBLANES, D_SHORT), F32)
        xbuf[0:SUBLANES, :] = jnp.zeros((SUBLANES, D_XBC), F32)
        st_ref[...] = jnp.zeros(st_ref.shape, F32)

    x = x_ref[...]
    hb = _rms_norm(x, gain_ref[...]).astype(BF16)

    pa = _dot(hb, wa_ref[...])
    u = pa[:, D_SHORT:2 * D_SHORT] * pa[:, 2 * D_SHORT:]
    ubuf[SUBLANES:SUBLANES + tm, :] = u
    va = cwa_ref[SHORT_CONV - 1:SHORT_CONV, :] * u
    for k in range(SHORT_CONV - 1):
        off = SUBLANES - (SHORT_CONV - 1) + k
        va = va + cwa_ref[k:k + 1, :] * ubuf[off:off + tm, :]
    ya = _dot((pa[:, :D_SHORT] * va).astype(BF16), swo_ref[...])
    ubuf[0:SUBLANES, :] = ubuf[tm:tm + SUBLANES, :]

    xbuf[SUBLANES:SUBLANES + tm, :] = _dot(hb, wxbc_ref[...])
    lane_sq = lax.broadcasted_iota(jnp.int32, (LANES, LANES), 1)
    cblk = 512
    for cb in range(D_XBC // cblk):
        cols = slice(cb * cblk, (cb + 1) * cblk)
        acc = cbb_ref[:, cols] + cwb_ref[SSM_CONV - 1:SSM_CONV, cols] * xbuf[SUBLANES:SUBLANES + tm, cols]
        for k in range(SSM_CONV - 1):
            off = SUBLANES - (SSM_CONV - 1) + k
            acc = acc + cwb_ref[k:k + 1, cols] * xbuf[off:off + tm, cols]
        v = acc * _sigmoid(acc)
        lo = cb * cblk
        if lo < D_INNER:
            xs_ref[:, cols] = v
        elif lo < D_INNER + SSM_GROUPS * D_STATE:
            for q in range(cblk // D_STATE):
                g = (lo - D_INNER) // D_STATE + q
                bt = v[:, q * D_STATE:(q + 1) * D_STATE].T
                for blk in range(tm // LANES):
                    w = bt[:, blk * LANES:(blk + 1) * LANES]
                    r = pltpu.roll(w, CHUNK, axis=1)
                    first = jnp.where(lane_sq < CHUNK, w, r)
                    second = jnp.where(lane_sq < CHUNK, r, w)
                    bmtd_ref[g, :, (2 * blk) * LANES:(2 * blk + 1) * LANES] = first.astype(BF16)
                    bmtd_ref[g, :, (2 * blk + 1) * LANES:(2 * blk + 2) * LANES] = second.astype(BF16)
        else:
            c0 = lo - D_INNER - SSM_GROUPS * D_STATE
            cm_ref[:, c0:c0 + cblk] = v.astype(BF16)
    xbuf[0:SUBLANES, :] = xbuf[tm:tm + SUBLANES, :]

    pre = _dot(hb, wdt_ref[...]) + dtb_ref[...]
    dt = jnp.maximum(pre, 0.0) + jnp.log1p(jnp.exp(-jnp.abs(pre)))
    a = dt * (-jnp.exp(alog_ref[...]))
    a_hi = a.astype(BF16)
    a_r1 = a - a_hi.astype(F32)
    a_mid = a_r1.astype(BF16)
    a_lo = (a_r1 - a_mid.astype(F32)).astype(BF16)
    tril = tril_ref[...]
    cs = _dot(tril, a_hi) + _dot(tril, a_mid) + _dot(tril, a_lo)
    cs_t = cs.T
    for blk in range(tm // LANES):
        cst_ref[blk] = cs_t[:, blk * LANES:(blk + 1) * LANES]
    cs_hi, cs_lo = _split2(cs)
    dt_hi, dt_lo = _split2(dt)
    lane_t = lax.broadcasted_iota(jnp.int32, (tm, LANES), 1)
    packed = jnp.where(lane_t < SSM_HEADS, cs_hi,
                       jnp.where(lane_t < 2 * SSM_HEADS, cs_lo,
                                 jnp.where(lane_t < 3 * SSM_HEADS, dt_hi, dt_lo)))
    expanded = _dot(packed.astype(BF16), expand_ref[...])
    cscol_ref[...] = expanded[:, :D_INNER]
    dt_exp = expanded[:, D_INNER:]

    for c in range(n_chunks):
        rows = slice(c * CHUNK, (c + 1) * CHUNK)
        cse = cscol_ref[rows, :]
        xdt = xs_ref[rows, :] * dt_exp[rows, :]
        xdt_ref[rows, :] = xdt.astype(BF16)
        xpr_ref[rows, :] = (xdt * jnp.exp(cse[CHUNK - 1:CHUNK, :] - cse)).astype(BF16)

    lane_8 = lax.broadcasted_iota(jnp.int32, (SSM_GROUPS, LANES), 1)
    row_i = lax.broadcasted_iota(jnp.int32, (CHUNK, GROUP_WIDTH), 0)
    lane_j = lax.broadcasted_iota(jnp.int32, (CHUNK, GROUP_WIDTH), 1) % CHUNK
    causal = row_i >= lane_j
    bdmask = bdmask_ref[...]

    for c in range(n_chunks):
        rows = slice(c * CHUNK, (c + 1) * CHUNK)
        blk = c // 2
        pieces = [cst_ref[blk, pl.ds(hh, SSM_GROUPS, stride=HEADS_PER_GROUP), :]
                  for hh in range(HEADS_PER_GROUP)]
        rolled = [pltpu.roll(p, CHUNK, axis=1) for p in pieces]
        if c % 2 == 0:
            halves = [jnp.where(lane_8 < CHUNK, pieces[0], rolled[1]),
                      jnp.where(lane_8 < CHUNK, pieces[2], rolled[3])]
        else:
            halves = [jnp.where(lane_8 < CHUNK, rolled[0], pieces[1]),
                      jnp.where(lane_8 < CHUNK, rolled[2], pieces[3])]
        cs_row = jnp.concatenate(halves, axis=1)

        for g in range(SSM_GROUPS):
            gcols = slice(g * GROUP_WIDTH, (g + 1) * GROUP_WIDTH)
            bt = bmtd_ref[g, :, c * LANES:(c + 1) * LANES]
            cmb = cm_ref[rows, g * D_STATE:(g + 1) * D_STATE]
            cb4 = _dot(cmb, jnp.concatenate([bt, bt], axis=1))
            col = cscol_ref[rows, gcols]
            decay = jnp.exp(jnp.where(causal, col - cs_row[g:g + 1, :], MASKED))
            m = (cb4 * decay).astype(BF16)
            xd = xdt_ref[rows, gcols]
            bd = jnp.concatenate([xd] * HEADS_PER_GROUP, axis=0) * bdmask
            y_diag = _dot(m, bd)
            state = st_ref[g]
            y_off = _dot(cmb, state.astype(BF16)) * jnp.exp(col)
            xs_ref[rows, gcols] = y_diag + y_off + xs_ref[rows, gcols] * dexp_ref[:, gcols]
            xp = xpr_ref[rows, gcols]
            st_ref[g] = (state * jnp.exp(col[CHUNK - 1:CHUNK, :])
                         + _dot(bt, jnp.concatenate([xp, jnp.zeros_like(xp)], axis=0)))

    z = _dot(hb, wz_ref[...])
    yz = xs_ref[...] * (z * _sigmoid(z))
    normed = []
    for g in range(SSM_GROUPS):
        yg = yz[:, g * GROUP_WIDTH:(g + 1) * GROUP_WIDTH]
        normed.append(yg * lax.rsqrt(jnp.mean(yg * yg, axis=-1, keepdims=True) + NORM_EPS))
    yn = (jnp.concatenate(normed, axis=1) * snorm_ref[...]).astype(BF16)
    yb = _dot(yn, sswo_ref[...])

    pg = _dot(hb, wg_ref[...])
    merged = _sigmoid(pg[:, :D_MODEL]) * ya + _sigmoid(pg[:, D_MODEL:]) * yb
    o_ref[...] = x + _dot(merged.astype(BF16), wo_ref[...])


def _mixer_constants(tm):
    t = np.arange(tm)
    tril = ((t[:, None] >= t[None, :]) & (t[:, None] // CHUNK == t[None, :] // CHUNK))
    k = np.arange(LANES)[:, None]
    col = np.arange(2 * D_INNER)[None, :]
    head = (col % D_INNER) // SSM_HEAD_DIM
    half = col // D_INNER
    expand = (k % SSM_HEADS == head) & ((k // (2 * SSM_HEADS)) == half)
    r = np.arange(GROUP_WIDTH)
    bdmask = (r[:, None] // SSM_HEAD_DIM) == (r[None, :] // SSM_HEAD_DIM)
    return (jnp.asarray(tril, BF16), jnp.asarray(expand, BF16), jnp.asarray(bdmask, BF16))


def _mixer(x2d, batch, seq, gain, w_in, conv_a, short_w_out, conv_b, conv_b_bias, dt_bias, a_log,
           d_skip, ssm_norm, ssm_w_out, w_out, *, tm=256):
    d = D_MODEL
    ns = seq // tm
    c0 = 3 * D_SHORT
    c1 = c0 + D_INNER
    c2 = c1 + D_XBC
    c3 = c2 + SSM_HEADS
    w_bf = w_in.astype(BF16)
    wa, wz, wxbc, wg = w_bf[:, :c0], w_bf[:, c0:c1], w_bf[:, c1:c2], w_bf[:, c3:]
    wdt = jnp.tile(w_bf[:, c2:c3], (1, HEAD_COPIES))
    dtb = jnp.tile(dt_bias, HEAD_COPIES).reshape(1, LANES)
    alog = jnp.tile(a_log, HEAD_COPIES).reshape(1, LANES)
    dexp = jnp.repeat(d_skip, SSM_HEAD_DIM).reshape(1, D_INNER)
    tril, expand, bdmask = _mixer_constants(tm)
    operands = [
        gain.reshape(1, d), wa, wz, wxbc, wdt, wg,
        conv_a, short_w_out.astype(BF16), conv_b, conv_b_bias.reshape(1, D_XBC), dtb, alog, dexp,
        ssm_norm.reshape(1, D_INNER), ssm_w_out.astype(BF16), w_out.astype(BF16),
        tril, expand, bdmask,
    ]
    x_spec = pl.BlockSpec((tm, d), lambda b, s: (b * ns + s, 0))
    scratch = [
        pltpu.VMEM((SUBLANES + tm, D_SHORT), F32),
        pltpu.VMEM((SUBLANES + tm, D_XBC), F32),
        pltpu.VMEM((SSM_GROUPS, D_STATE, GROUP_WIDTH), F32),
        pltpu.VMEM((tm, D_INNER), F32),
        pltpu.VMEM((tm, D_INNER), F32),
        pltpu.VMEM((tm, D_INNER), BF16),
        pltpu.VMEM((tm, D_INNER), BF16),
        pltpu.VMEM((tm, SSM_GROUPS * D_STATE), BF16),
        pltpu.VMEM((SSM_GROUPS, D_STATE, 2 * tm), BF16),
        pltpu.VMEM((tm // LANES, LANES, LANES), F32),
    ]
    return pl.pallas_call(
        functools.partial(_mixer_body, tm=tm),
        grid=(batch, ns),
        in_specs=[x_spec] + [_resident(op.shape) for op in operands],
        out_specs=x_spec,
        out_shape=jax.ShapeDtypeStruct(x2d.shape, F32),
        scratch_shapes=scratch,
        compiler_params=pltpu.CompilerParams(
            dimension_semantics=("arbitrary", "arbitrary"), vmem_limit_bytes=VMEM_LIMIT_BYTES),
        name="mixer",
    )(x2d, *operands)


def kernel(x, ffn1_norm, ffn1_w_in, ffn1_w_out, mix_norm, w_in, short_conv_w, short_w_out, ssm_conv_w, ssm_conv_b, ssm_dt_bias, ssm_A_log, ssm_D, ssm_norm, ssm_w_out, w_out, ffn2_norm, ffn2_w_in, ffn2_w_out, final_norm):
    b, s, d = x.shape
    x2d = x.reshape(b * s, d)
    fin = final_norm.reshape(1, d)
    for l in range(ffn1_norm.shape[0]):
        x2d = _ffn(x2d, ffn1_norm[l].reshape(1, d), ffn1_w_in[l].astype(BF16),
                   ffn1_w_out[l].astype(BF16), fin, final_norm=False)
        x2d = _mixer(x2d, b, s, mix_norm[l], w_in[l], short_conv_w[l], short_w_out[l],
                     ssm_conv_w[l], ssm_conv_b[l], ssm_dt_bias[l], ssm_A_log[l], ssm_D[l],
                     ssm_norm[l], ssm_w_out[l], w_out[l])
        last = l == ffn1_norm.shape[0] - 1
        x2d = _ffn(x2d, ffn2_norm[l].reshape(1, d), ffn2_w_in[l].astype(BF16),
                   ffn2_w_out[l].astype(BF16), fin, final_norm=last)
    return x2d.reshape(b, s, d)
```

```python
import functools

import jax
import jax.numpy as jnp
import numpy as np
from jax import lax
from jax.experimental import pallas as pl
from jax.experimental.pallas import tpu as pltpu

D_MODEL = 1024
D_FF = 2816
D_SHORT = D_MODEL
SHORT_CONV = 3
D_INNER = 2048
SSM_HEADS = 32
SSM_HEAD_DIM = 64
SSM_GROUPS = 8
D_STATE = 128
SSM_CONV = 4
D_XBC = D_INNER + 2 * SSM_GROUPS * D_STATE
CHUNK = 64
NORM_EPS = 1e-5

HEADS_PER_GROUP = SSM_HEADS // SSM_GROUPS
GROUP_WIDTH = HEADS_PER_GROUP * SSM_HEAD_DIM
GROUP_XBC = GROUP_WIDTH + 2 * D_STATE
LANES = 128
SUBLANES = 8
HEAD_COPIES = LANES // SSM_HEADS
A_BLOCK = 256
N_A_BLOCKS = D_SHORT // A_BLOCK

F32 = jnp.float32
BF16 = jnp.bfloat16

VMEM_LIMIT_BYTES = 58 * 1024 * 1024
MASKED = -1e30


def _rms_norm(x, gain):
    ms = jnp.mean(x * x, axis=-1, keepdims=True)
    return x * lax.rsqrt(ms + NORM_EPS) * gain


def _sigmoid(x):
    return 1.0 / (1.0 + jnp.exp(-x))


def _dot(a, b):
    return jnp.dot(a, b, preferred_element_type=F32)


def _split2(x):
    hi = x.astype(BF16).astype(F32)
    lo = (x - hi).astype(BF16).astype(F32)
    return hi, lo


def _resident(shape):
    nd = len(shape)
    return pl.BlockSpec(shape, lambda *_: (0,) * nd, pipeline_mode=pl.Buffered(1))


def _ffn_body(x_ref, gain_ref, w_in_ref, w_out_ref, fin_ref, o_ref, *, ff_chunk, final_norm):
    x = x_ref[...]
    h = _rms_norm(x, gain_ref[...]).astype(BF16)
    acc = jnp.zeros(x.shape, F32)
    for j in range(D_FF // ff_chunk):
        lo = j * ff_chunk
        g = _dot(h, w_in_ref[:, lo:lo + ff_chunk])
        u = _dot(h, w_in_ref[:, D_FF + lo:D_FF + lo + ff_chunk])
        a = (g * _sigmoid(g) * u).astype(BF16)
        acc = acc + _dot(a, w_out_ref[lo:lo + ff_chunk, :])
    y = x + 0.5 * acc
    if final_norm:
        y = _rms_norm(y, fin_ref[...])
    o_ref[...] = y


def _ffn(x2d, gain, w_in, w_out, fin, *, final_norm, tm=512, ff_chunk=256):
    t, d = x2d.shape
    body = functools.partial(_ffn_body, ff_chunk=ff_chunk, final_norm=final_norm)
    return pl.pallas_call(
        body,
        grid=(t // tm,),
        in_specs=[
            pl.BlockSpec((tm, d), lambda i: (i, 0)),
            _resident((1, d)),
            _resident(w_in.shape),
            _resident(w_out.shape),
            _resident((1, d)),
        ],
        out_specs=pl.BlockSpec((tm, d), lambda i: (i, 0)),
        out_shape=jax.ShapeDtypeStruct((t, d), F32),
        compiler_params=pltpu.CompilerParams(
            dimension_semantics=("arbitrary",), vmem_limit_bytes=VMEM_LIMIT_BYTES),
        name="ffn_final" if final_norm else "ffn",
    )(x2d, gain, w_in, w_out, fin)


def _mixer_body(x_ref, gain_ref, wa_ref, wz_ref, wxbc_ref, wdt_ref, wg_ref,
                cwa_ref, swo_ref, cwb_ref, cbb_ref, dtb_ref, alog_ref, dexp_ref,
                snorm_ref, sswo_ref, wo_ref, tril_ref, expand_ref, bdmask_ref,
                o_ref,
                ubuf, xbuf, zbuf, st_ref, cst_ref, yap_ref, yn_ref, mg_ref,
                *, tm):
    n_chunks = tm // CHUNK

    @pl.when(pl.program_id(1) == 0)
    def _():
        ubuf[:, 0:SUBLANES, :] = jnp.zeros((N_A_BLOCKS, SUBLANES, A_BLOCK), F32)
        xbuf[:, 0:SUBLANES, :] = jnp.zeros((SSM_GROUPS, SUBLANES, GROUP_XBC), F32)
        st_ref[...] = jnp.zeros(st_ref.shape, F32)

    x = x_ref[...]
    hb = _rms_norm(x, gain_ref[...]).astype(BF16)

    pre = _dot(hb, wdt_ref[...]) + dtb_ref[...]
    dt = jnp.maximum(pre, 0.0) + jnp.log1p(jnp.exp(-jnp.abs(pre)))
    a = dt * (-jnp.exp(alog_ref[...]))
    a_hi = a.astype(BF16)
    a_r1 = a - a_hi.astype(F32)
    a_mid = a_r1.astype(BF16)
    a_lo = (a_r1 - a_mid.astype(F32)).astype(BF16)
    tril = tril_ref[...]
    cs = _dot(tril, a_hi) + _dot(tril, a_mid) + _dot(tril, a_lo)
    cs_t = cs.T
    for blk in range(tm // LANES):
        cst_ref[blk] = cs_t[:, blk * LANES:(blk + 1) * LANES]
    cs_hi, cs_lo = _split2(cs)
    dt_hi, dt_lo = _split2(dt)
    lane_t = lax.broadcasted_iota(jnp.int32, (tm, LANES), 1)
    packed = jnp.where(lane_t < SSM_HEADS, cs_hi,
                       jnp.where(lane_t < 2 * SSM_HEADS, cs_lo,
                                 jnp.where(lane_t < 3 * SSM_HEADS, dt_hi, dt_lo))).astype(BF16)

    lane_8 = lax.broadcasted_iota(jnp.int32, (SSM_GROUPS, LANES), 1)
    cs_rows = []
    for c in range(n_chunks):
        blk = c // 2
        pieces = [cst_ref[blk, pl.ds(hh, SSM_GROUPS, stride=HEADS_PER_GROUP), :]
                  for hh in range(HEADS_PER_GROUP)]
        rolled = [pltpu.roll(p, CHUNK, axis=1) for p in pieces]
        if c % 2 == 0:
            halves = [jnp.where(lane_8 < CHUNK, pieces[0], rolled[1]),
                      jnp.where(lane_8 < CHUNK, pieces[2], rolled[3])]
        else:
            halves = [jnp.where(lane_8 < CHUNK, rolled[0], pieces[1]),
                      jnp.where(lane_8 < CHUNK, rolled[2], pieces[3])]
        cs_rows.append(jnp.concatenate(halves, axis=1))

    def project_a(j):
        return _dot(hb, wa_ref[:, 3 * j * A_BLOCK:3 * (j + 1) * A_BLOCK])

    def project_xbc(g):
        xbuf[g, SUBLANES:SUBLANES + tm, :] = _dot(hb, wxbc_ref[:, g * GROUP_XBC:(g + 1) * GROUP_XBC])
        zbuf[g] = _dot(hb, wz_ref[:, g * GROUP_WIDTH:(g + 1) * GROUP_WIDTH])

    p_next = project_a(0)
    for j in range(N_A_BLOCKS):
        cols = slice(j * A_BLOCK, (j + 1) * A_BLOCK)
        p = p_next
        if j + 1 < N_A_BLOCKS:
            p_next = project_a(j + 1)
        else:
            project_xbc(0)
        u = p[:, A_BLOCK:2 * A_BLOCK] * p[:, 2 * A_BLOCK:]
        ubuf[j, SUBLANES:SUBLANES + tm, :] = u
        va = cwa_ref[SHORT_CONV - 1:SHORT_CONV, cols] * u
        for k in range(SHORT_CONV - 1):
            off = SUBLANES - (SHORT_CONV - 1) + k
            va = va + cwa_ref[k:k + 1, cols] * ubuf[j, off:off + tm, :]
        ubuf[j, 0:SUBLANES, :] = ubuf[j, tm:tm + SUBLANES, :]
        yap_ref[j] = (p[:, :A_BLOCK] * va).astype(BF16)

    lane_sq = lax.broadcasted_iota(jnp.int32, (LANES, LANES), 1)
    row_i = lax.broadcasted_iota(jnp.int32, (CHUNK, GROUP_WIDTH), 0)
    lane_j = lax.broadcasted_iota(jnp.int32, (CHUNK, GROUP_WIDTH), 1) % CHUNK
    causal = row_i >= lane_j
    bdmask = bdmask_ref[...]

    ya = None
    for g in range(SSM_GROUPS):
        xcols = slice(g * GROUP_XBC, (g + 1) * GROUP_XBC)
        gcols = slice(g * GROUP_WIDTH, (g + 1) * GROUP_WIDTH)
        if g + 1 < SSM_GROUPS:
            project_xbc(g + 1)
        if g == 0:
            ya = _dot(jnp.concatenate([yap_ref[j] for j in range(N_A_BLOCKS)], axis=1), swo_ref[...])
        acc = cbb_ref[:, xcols] + cwb_ref[SSM_CONV - 1:SSM_CONV, xcols] * xbuf[g, SUBLANES:SUBLANES + tm, :]
        for k in range(SSM_CONV - 1):
            off = SUBLANES - (SSM_CONV - 1) + k
            acc = acc + cwb_ref[k:k + 1, xcols] * xbuf[g, off:off + tm, :]
        xbuf[g, 0:SUBLANES, :] = xbuf[g, tm:tm + SUBLANES, :]
        v = acc * _sigmoid(acc)
        xs = v[:, :GROUP_WIDTH]
        cmg = v[:, GROUP_WIDTH + D_STATE:].astype(BF16)
        bm_t = v[:, GROUP_WIDTH:GROUP_WIDTH + D_STATE].T
        bt = []
        for blk in range(tm // LANES):
            w = bm_t[:, blk * LANES:(blk + 1) * LANES]
            r = pltpu.roll(w, CHUNK, axis=1)
            bt.append(jnp.where(lane_sq < CHUNK, w, r).astype(BF16))
            bt.append(jnp.where(lane_sq < CHUNK, r, w).astype(BF16))

        ex = _dot(packed, expand_ref[:, 2 * g * GROUP_WIDTH:2 * (g + 1) * GROUP_WIDTH])
        cscol = ex[:, :GROUP_WIDTH]
        xdt = xs * ex[:, GROUP_WIDTH:]

        state = st_ref[g]
        ys = []
        for c in range(n_chunks):
            rows = slice(c * CHUNK, (c + 1) * CHUNK)
            col = cscol[rows]
            xdt_c = xdt[rows]
            cm_c = cmg[rows]
            cb4 = _dot(cm_c, jnp.concatenate([bt[c], bt[c]], axis=1))
            decay = jnp.exp(jnp.where(causal, col - cs_rows[c][g:g + 1, :], MASKED))
            m = (cb4 * decay).astype(BF16)
            xd = xdt_c.astype(BF16)
            bd = jnp.concatenate([xd] * HEADS_PER_GROUP, axis=0) * bdmask
            y_off = _dot(cm_c, state.astype(BF16)) * jnp.exp(col)
            ys.append(_dot(m, bd) + y_off + xs[rows] * dexp_ref[:, gcols])
            last = col[CHUNK - 1:CHUNK, :]
            xp = (xdt_c * jnp.exp(last - col)).astype(BF16)
            state = state * jnp.exp(last) + _dot(bt[c], jnp.concatenate([xp, jnp.zeros_like(xp)], axis=0))
        st_ref[g] = state
        y = jnp.concatenate(ys, axis=0)

        z = zbuf[g]
        yz = y * (z * _sigmoid(z))
        yn = yz * lax.rsqrt(jnp.mean(yz * yz, axis=-1, keepdims=True) + NORM_EPS) * snorm_ref[:, gcols]
        yn_ref[g] = yn.astype(BF16)

    def project_gates(j):
        return _dot(hb, wg_ref[:, 2 * j * A_BLOCK:2 * (j + 1) * A_BLOCK])

    pg_next = project_gates(0)
    yb = _dot(jnp.concatenate([yn_ref[g] for g in range(SSM_GROUPS)], axis=1), sswo_ref[...])
    for j in range(N_A_BLOCKS):
        cols = slice(j * A_BLOCK, (j + 1) * A_BLOCK)
        pg = pg_next
        if j + 1 < N_A_BLOCKS:
            pg_next = project_gates(j + 1)
        merged = _sigmoid(pg[:, :A_BLOCK]) * ya[:, cols] + _sigmoid(pg[:, A_BLOCK:]) * yb[:, cols]
        mg_ref[j] = merged.astype(BF16)
    o_ref[...] = x + _dot(jnp.concatenate([mg_ref[j] for j in range(N_A_BLOCKS)], axis=1), wo_ref[...])


def _mixer_constants(tm):
    t = np.arange(tm)
    tril = ((t[:, None] >= t[None, :]) & (t[:, None] // CHUNK == t[None, :] // CHUNK))
    k = np.arange(LANES)[:, None]
    col = np.arange(2 * D_INNER)[None, :]
    group, within = col // (2 * GROUP_WIDTH), col % (2 * GROUP_WIDTH)
    head = group * HEADS_PER_GROUP + (within % GROUP_WIDTH) // SSM_HEAD_DIM
    half = within // GROUP_WIDTH
    expand = (k % SSM_HEADS == head) & ((k // (2 * SSM_HEADS)) == half)
    r = np.arange(GROUP_WIDTH)
    bdmask = (r[:, None] // SSM_HEAD_DIM) == (r[None, :] // SSM_HEAD_DIM)
    return (jnp.asarray(tril, BF16), jnp.asarray(expand, BF16), jnp.asarray(bdmask, BF16))


def _interleave(parts, block):
    rows = parts[0].shape[:-1]
    stacked = jnp.stack([p.reshape(rows + (-1, block)) for p in parts], axis=-2)
    return stacked.reshape(rows + (-1,))


def _group_xbc(a):
    rows = a.shape[:-1]
    xs = a[..., :D_INNER].reshape(rows + (SSM_GROUPS, GROUP_WIDTH))
    bm = a[..., D_INNER:D_INNER + SSM_GROUPS * D_STATE].reshape(rows + (SSM_GROUPS, D_STATE))
    cm = a[..., D_INNER + SSM_GROUPS * D_STATE:].reshape(rows + (SSM_GROUPS, D_STATE))
    return jnp.concatenate([xs, bm, cm], axis=-1).reshape(rows + (D_XBC,))


def _mixer(x2d, batch, seq, gain, w_in, conv_a, short_w_out, conv_b, conv_b_bias, dt_bias, a_log,
           d_skip, ssm_norm, ssm_w_out, w_out, *, tm=256):
    d = D_MODEL
    ns = seq // tm
    c0 = 3 * D_SHORT
    c1 = c0 + D_INNER
    c2 = c1 + D_XBC
    c3 = c2 + SSM_HEADS
    wa = _interleave([w_in[:, :D_SHORT], w_in[:, D_SHORT:2 * D_SHORT], w_in[:, 2 * D_SHORT:c0]],
                     A_BLOCK).astype(BF16)
    wz = w_in[:, c0:c1].astype(BF16)
    wxbc = _group_xbc(w_in[:, c1:c2]).astype(BF16)
    wdt = jnp.tile(w_in[:, c2:c3], (1, HEAD_COPIES)).astype(BF16)
    wg = _interleave([w_in[:, c3:c3 + D_MODEL], w_in[:, c3 + D_MODEL:]], A_BLOCK).astype(BF16)
    dtb = jnp.tile(dt_bias, HEAD_COPIES).reshape(1, LANES)
    alog = jnp.tile(a_log, HEAD_COPIES).reshape(1, LANES)
    dexp = jnp.repeat(d_skip, SSM_HEAD_DIM).reshape(1, D_INNER)
    tril, expand, bdmask = _mixer_constants(tm)
    operands = [
        gain.reshape(1, d), wa, wz, wxbc, wdt, wg,
        conv_a, short_w_out.astype(BF16), _group_xbc(conv_b), _group_xbc(conv_b_bias).reshape(1, D_XBC),
        dtb, alog, dexp,
        ssm_norm.reshape(1, D_INNER), ssm_w_out.astype(BF16), w_out.astype(BF16),
        tril, expand, bdmask,
    ]
    x_spec = pl.BlockSpec((tm, d), lambda b, s: (b * ns + s, 0))
    scratch = [
        pltpu.VMEM((N_A_BLOCKS, SUBLANES + tm, A_BLOCK), F32),
        pltpu.VMEM((SSM_GROUPS, SUBLANES + tm, GROUP_XBC), F32),
        pltpu.VMEM((SSM_GROUPS, tm, GROUP_WIDTH), F32),
        pltpu.VMEM((SSM_GROUPS, D_STATE, GROUP_WIDTH), F32),
        pltpu.VMEM((tm // LANES, LANES, LANES), F32),
        pltpu.VMEM((N_A_BLOCKS, tm, A_BLOCK), BF16),
        pltpu.VMEM((SSM_GROUPS, tm, GROUP_WIDTH), BF16),
        pltpu.VMEM((N_A_BLOCKS, tm, A_BLOCK), BF16),
    ]
    return pl.pallas_call(
        functools.partial(_mixer_body, tm=tm),
        grid=(batch, ns),
        in_specs=[x_spec] + [_resident(op.shape) for op in operands],
        out_specs=x_spec,
        out_shape=jax.ShapeDtypeStruct(x2d.shape, F32),
        scratch_shapes=scratch,
        compiler_params=pltpu.CompilerParams(
            dimension_semantics=("arbitrary", "arbitrary"), vmem_limit_bytes=VMEM_LIMIT_BYTES),
        name="mixer",
    )(x2d, *operands)


def kernel(x, ffn1_norm, ffn1_w_in, ffn1_w_out, mix_norm, w_in, short_conv_w, short_w_out, ssm_conv_w, ssm_conv_b, ssm_dt_bias, ssm_A_log, ssm_D, ssm_norm, ssm_w_out, w_out, ffn2_norm, ffn2_w_in, ffn2_w_out, final_norm):
    b, s, d = x.shape
    x2d = x.reshape(b * s, d)
    fin = final_norm.reshape(1, d)
    for l in range(ffn1_norm.shape[0]):
        x2d = _ffn(x2d, ffn1_norm[l].reshape(1, d), ffn1_w_in[l].astype(BF16),
                   ffn1_w_out[l].astype(BF16), fin, final_norm=False)
        x2d = _mixer(x2d, b, s, mix_norm[l], w_in[l], short_conv_w[l], short_w_out[l],
                     ssm_conv_w[l], ssm_conv_b[l], ssm_dt_bias[l], ssm_A_log[l], ssm_D[l],
                     ssm_norm[l], ssm_w_out[l], w_out[l])
        last = l == ffn1_norm.shape[0] - 1
        x2d = _ffn(x2d, ffn2_norm[l].reshape(1, d), ffn2_w_in[l].astype(BF16),
                   ffn2_w_out[l].astype(BF16), fin, final_norm=last)
    return x2d.reshape(b, s, d)
```

```python
import functools

import jax
import jax.numpy as jnp
import numpy as np
from jax import lax
from jax.experimental import pallas as pl
from jax.experimental.pallas import tpu as pltpu

D_MODEL = 1024
D_FF = 2816
D_SHORT = D_MODEL
SHORT_CONV = 3
D_INNER = 2048
SSM_HEADS = 32
SSM_HEAD_DIM = 64
SSM_GROUPS = 8
D_STATE = 128
SSM_CONV = 4
D_XBC = D_INNER + 2 * SSM_GROUPS * D_STATE
CHUNK = 64
NORM_EPS = 1e-5

HEADS_PER_GROUP = SSM_HEADS // SSM_GROUPS
GROUP_WIDTH = HEADS_PER_GROUP * SSM_HEAD_DIM
GROUP_XBC = GROUP_WIDTH + 2 * D_STATE
LANES = 128
SUBLANES = 8
HEAD_COPIES = LANES // SSM_HEADS
A_BLOCK = 256
N_A_BLOCKS = D_SHORT // A_BLOCK
GROUPS_IN_FLIGHT = 2
assert SSM_GROUPS // GROUPS_IN_FLIGHT == N_A_BLOCKS
W_Z_START = 3 * D_SHORT
W_XBC_START = W_Z_START + D_INNER

F32 = jnp.float32
BF16 = jnp.bfloat16

VMEM_LIMIT_BYTES = 58 * 1024 * 1024
MASKED = -1e30


def _rms_norm(x, gain):
    ms = jnp.mean(x * x, axis=-1, keepdims=True)
    return x * lax.rsqrt(ms + NORM_EPS) * gain


def _sigmoid(x):
    return 1.0 / (1.0 + jnp.exp(-x))


def _dot(a, b):
    return jnp.dot(a, b, preferred_element_type=F32)


def _split2(x):
    hi = x.astype(BF16).astype(F32)
    lo = (x - hi).astype(BF16).astype(F32)
    return hi, lo


def _resident(shape):
    nd = len(shape)
    return pl.BlockSpec(shape, lambda *_: (0,) * nd, pipeline_mode=pl.Buffered(1))


def _ffn_body(x_ref, gain_ref, w_in_ref, w_out_ref, fin_ref, o_ref, *, ff_chunk, final_norm):
    x = x_ref[...]
    h = _rms_norm(x, gain_ref[...]).astype(BF16)
    acc = jnp.zeros(x.shape, F32)
    for j in range(D_FF // ff_chunk):
        lo = j * ff_chunk
        g = _dot(h, w_in_ref[:, lo:lo + ff_chunk])
        u = _dot(h, w_in_ref[:, D_FF + lo:D_FF + lo + ff_chunk])
        a = (g * _sigmoid(g) * u).astype(BF16)
        acc = acc + _dot(a, w_out_ref[lo:lo + ff_chunk, :])
    y = x + 0.5 * acc
    if final_norm:
        y = _rms_norm(y, fin_ref[...])
    o_ref[...] = y


def _ffn(x2d, gain, w_in, w_out, fin, *, final_norm, tm=512, ff_chunk=256):
    t, d = x2d.shape
    body = functools.partial(_ffn_body, ff_chunk=ff_chunk, final_norm=final_norm)
    return pl.pallas_call(
        body,
        grid=(t // tm,),
        in_specs=[
            pl.BlockSpec((tm, d), lambda i: (i, 0)),
            _resident((1, d)),
            _resident(w_in.shape),
            _resident(w_out.shape),
            _resident((1, d)),
        ],
        out_specs=pl.BlockSpec((tm, d), lambda i: (i, 0)),
        out_shape=jax.ShapeDtypeStruct((t, d), F32),
        compiler_params=pltpu.CompilerParams(
            dimension_semantics=("arbitrary",), vmem_limit_bytes=VMEM_LIMIT_BYTES),
        name="ffn_final" if final_norm else "ffn",
    )(x2d, gain, w_in, w_out, fin)


def _mixer_body(x_ref, gain_ref, wmain_ref, wdt_ref, wg_ref,
                cwa_ref, swo_ref, cwb_ref, cbb_ref, dtb_ref, alog_ref, dexp_ref,
                snorm_ref, sswo_ref, wo_ref, tril_ref, expand_ref, bdmask_ref,
                o_ref,
                hb_ref, ubuf, xbuf, zbuf, st_ref, cst_ref, yap_ref, yn_ref, mg_ref, pgbuf, yabuf, ybbuf,
                *, tm):
    n_chunks = tm // CHUNK

    @pl.when(pl.program_id(1) == 0)
    def _():
        ubuf[:, 0:SUBLANES, :] = jnp.zeros((N_A_BLOCKS, SUBLANES, A_BLOCK), F32)
        xbuf[:, 0:SUBLANES, :] = jnp.zeros((SSM_GROUPS, SUBLANES, GROUP_XBC), F32)
        st_ref[...] = jnp.zeros(st_ref.shape, F32)

    x = x_ref[...]
    hb_ref[...] = _rms_norm(x, gain_ref[...]).astype(BF16)

    def project(w_ref, pieces):
        w = [w_ref[:, lo:lo + width] for lo, width in pieces]
        return _dot(hb_ref[...], w[0] if len(w) == 1 else jnp.concatenate(w, axis=1))

    def xbc_pieces(g):
        return [(g * GROUP_WIDTH, GROUP_WIDTH),
                (D_INNER + g * D_STATE, D_STATE),
                (D_INNER + (SSM_GROUPS + g) * D_STATE, D_STATE)]

    def gather_xbc(ref, rows, g):
        return jnp.concatenate([ref[rows, lo:lo + width] for lo, width in xbc_pieces(g)], axis=1)

    pre = _dot(hb_ref[...], wdt_ref[...]) + dtb_ref[...]
    dt = jnp.maximum(pre, 0.0) + jnp.log1p(jnp.exp(-jnp.abs(pre)))
    a = dt * (-jnp.exp(alog_ref[...]))
    a_hi = a.astype(BF16)
    a_r1 = a - a_hi.astype(F32)
    a_mid = a_r1.astype(BF16)
    a_lo = (a_r1 - a_mid.astype(F32)).astype(BF16)
    tril = tril_ref[...]
    cs = _dot(tril, a_hi) + _dot(tril, a_mid) + _dot(tril, a_lo)
    cs_t = cs.T
    for blk in range(tm // LANES):
        cst_ref[blk] = cs_t[:, blk * LANES:(blk + 1) * LANES]
    cs_hi, cs_lo = _split2(cs)
    dt_hi, dt_lo = _split2(dt)
    lane_t = lax.broadcasted_iota(jnp.int32, (tm, LANES), 1)
    packed = jnp.where(lane_t < SSM_HEADS, cs_hi,
                       jnp.where(lane_t < 2 * SSM_HEADS, cs_lo,
                                 jnp.where(lane_t < 3 * SSM_HEADS, dt_hi, dt_lo))).astype(BF16)

    lane_8 = lax.broadcasted_iota(jnp.int32, (SSM_GROUPS, LANES), 1)
    cs_rows = []
    for c in range(n_chunks):
        blk = c // 2
        pieces = [cst_ref[blk, pl.ds(hh, SSM_GROUPS, stride=HEADS_PER_GROUP), :]
                  for hh in range(HEADS_PER_GROUP)]
        rolled = [pltpu.roll(p, CHUNK, axis=1) for p in pieces]
        if c % 2 == 0:
            halves = [jnp.where(lane_8 < CHUNK, pieces[0], rolled[1]),
                      jnp.where(lane_8 < CHUNK, pieces[2], rolled[3])]
        else:
            halves = [jnp.where(lane_8 < CHUNK, rolled[0], pieces[1]),
                      jnp.where(lane_8 < CHUNK, rolled[2], pieces[3])]
        cs_rows.append(jnp.concatenate(halves, axis=1))

    def project_a(j):
        return project(wmain_ref, [(part * D_SHORT + j * A_BLOCK, A_BLOCK) for part in range(3)])

    def project_xbc(g):
        xbuf[g, SUBLANES:SUBLANES + tm, :] = project(
            wmain_ref, [(W_XBC_START + lo, width) for lo, width in xbc_pieces(g)])
        zbuf[g] = project(wmain_ref, [(W_Z_START + g * GROUP_WIDTH, GROUP_WIDTH)])

    p_next = project_a(0)
    for j in range(N_A_BLOCKS):
        cols = slice(j * A_BLOCK, (j + 1) * A_BLOCK)
        p = p_next
        if j + 1 < N_A_BLOCKS:
            p_next = project_a(j + 1)
        else:
            for g in range(GROUPS_IN_FLIGHT):
                project_xbc(g)
        u = p[:, A_BLOCK:2 * A_BLOCK] * p[:, 2 * A_BLOCK:]
        ubuf[j, SUBLANES:SUBLANES + tm, :] = u
        va = cwa_ref[SHORT_CONV - 1:SHORT_CONV, cols] * u
        for k in range(SHORT_CONV - 1):
            off = SUBLANES - (SHORT_CONV - 1) + k
            va = va + cwa_ref[k:k + 1, cols] * ubuf[j, off:off + tm, :]
        ubuf[j, 0:SUBLANES, :] = ubuf[j, tm:tm + SUBLANES, :]
        yap_ref[j] = (p[:, :A_BLOCK] * va).astype(BF16)

    lane_sq = lax.broadcasted_iota(jnp.int32, (LANES, LANES), 1)
    row_i = lax.broadcasted_iota(jnp.int32, (CHUNK, GROUP_WIDTH), 0)
    lane_j = lax.broadcasted_iota(jnp.int32, (CHUNK, GROUP_WIDTH), 1) % CHUNK
    causal = row_i >= lane_j
    bdmask = bdmask_ref[...]

    def group_stages(g):
        gcols = slice(g * GROUP_WIDTH, (g + 1) * GROUP_WIDTH)
        acc = (gather_xbc(cbb_ref, slice(None), g)
               + gather_xbc(cwb_ref, slice(SSM_CONV - 1, SSM_CONV), g) * xbuf[g, SUBLANES:SUBLANES + tm, :])
        for k in range(SSM_CONV - 1):
            off = SUBLANES - (SSM_CONV - 1) + k
            acc = acc + gather_xbc(cwb_ref, slice(k, k + 1), g) * xbuf[g, off:off + tm, :]
        xbuf[g, 0:SUBLANES, :] = xbuf[g, tm:tm + SUBLANES, :]
        v = acc * _sigmoid(acc)
        yield
        xs = v[:, :GROUP_WIDTH]
        cmg = v[:, GROUP_WIDTH + D_STATE:].astype(BF16)
        bm_t = v[:, GROUP_WIDTH:GROUP_WIDTH + D_STATE].T
        bt = []
        for blk in range(tm // LANES):
            w = bm_t[:, blk * LANES:(blk + 1) * LANES]
            r = pltpu.roll(w, CHUNK, axis=1)
            bt.append(jnp.where(lane_sq < CHUNK, w, r).astype(BF16))
            bt.append(jnp.where(lane_sq < CHUNK, r, w).astype(BF16))

        ex = _dot(packed, expand_ref[:, 2 * g * GROUP_WIDTH:2 * (g + 1) * GROUP_WIDTH])
        cscol = ex[:, :GROUP_WIDTH]
        xdt = xs * ex[:, GROUP_WIDTH:]
        yield

        state = st_ref[g]
        ys = []
        for c in range(n_chunks):
            rows = slice(c * CHUNK, (c + 1) * CHUNK)
            col = cscol[rows]
            xdt_c = xdt[rows]
            cm_c = cmg[rows]
            cb4 = _dot(cm_c, jnp.concatenate([bt[c], bt[c]], axis=1))
            decay = jnp.exp(jnp.where(causal, col - cs_rows[c][g:g + 1, :], MASKED))
            m = (cb4 * decay).astype(BF16)
            xd = xdt_c.astype(BF16)
            bd = jnp.concatenate([xd] * HEADS_PER_GROUP, axis=0) * bdmask
            y_off = _dot(cm_c, state.astype(BF16)) * jnp.exp(col)
            ys.append(_dot(m, bd) + y_off + xs[rows] * dexp_ref[:, gcols])
            last = col[CHUNK - 1:CHUNK, :]
            xp = (xdt_c * jnp.exp(last - col)).astype(BF16)
            state = state * jnp.exp(last) + _dot(bt[c], jnp.concatenate([xp, jnp.zeros_like(xp)], axis=0))
            yield
        st_ref[g] = state
        y = jnp.concatenate(ys, axis=0)

        z = zbuf[g]
        yz = y * (z * _sigmoid(z))
        yn = yz * lax.rsqrt(jnp.mean(yz * yz, axis=-1, keepdims=True) + NORM_EPS) * snorm_ref[:, gcols]
        yn_ref[g] = yn.astype(BF16)

    def ssd_out_partial(g_lo, g_hi):
        yn = jnp.concatenate([yn_ref[g] for g in range(g_lo, g_hi)], axis=1)
        return _dot(yn, sswo_ref[g_lo * GROUP_WIDTH:g_hi * GROUP_WIDTH, :])

    def dense_stages(step):
        g0 = step * GROUPS_IN_FLIGHT
        for g in range(g0 + GROUPS_IN_FLIGHT, min(g0 + 2 * GROUPS_IN_FLIGHT, SSM_GROUPS)):
            project_xbc(g)
            yield
        if step < N_A_BLOCKS:
            pgbuf[step] = project(wg_ref, [(part * D_MODEL + step * A_BLOCK, A_BLOCK) for part in range(2)])
            yield
        if step == 0:
            yabuf[...] = _dot(jnp.concatenate([yap_ref[j] for j in range(N_A_BLOCKS)], axis=1), swo_ref[...])
        elif step == 1:
            ybbuf[...] = ssd_out_partial(0, g0)
        else:
            ybbuf[...] += ssd_out_partial(g0 - GROUPS_IN_FLIGHT, g0)
        yield

    for step in range(SSM_GROUPS // GROUPS_IN_FLIGHT):
        g0 = step * GROUPS_IN_FLIGHT
        live = [group_stages(g) for g in range(g0, g0 + GROUPS_IN_FLIGHT)] + [dense_stages(step)]
        while live:
            live = [stages for stages in live if next(stages, True) is None]
    ybbuf[...] += ssd_out_partial(SSM_GROUPS - GROUPS_IN_FLIGHT, SSM_GROUPS)

    for j in range(N_A_BLOCKS):
        cols = slice(j * A_BLOCK, (j + 1) * A_BLOCK)
        pg = pgbuf[j]
        merged = _sigmoid(pg[:, :A_BLOCK]) * yabuf[:, cols] + _sigmoid(pg[:, A_BLOCK:]) * ybbuf[:, cols]
        mg_ref[j] = merged.astype(BF16)
    o_ref[...] = x + _dot(jnp.concatenate([mg_ref[j] for j in range(N_A_BLOCKS)], axis=1), wo_ref[...])


def _mixer_constants(tm):
    t = np.arange(tm)
    tril = ((t[:, None] >= t[None, :]) & (t[:, None] // CHUNK == t[None, :] // CHUNK))
    k = np.arange(LANES)[:, None]
    col = np.arange(2 * D_INNER)[None, :]
    group, within = col // (2 * GROUP_WIDTH), col % (2 * GROUP_WIDTH)
    head = group * HEADS_PER_GROUP + (within % GROUP_WIDTH) // SSM_HEAD_DIM
    half = within // GROUP_WIDTH
    expand = (k % SSM_HEADS == head) & ((k // (2 * SSM_HEADS)) == half)
    r = np.arange(GROUP_WIDTH)
    bdmask = (r[:, None] // SSM_HEAD_DIM) == (r[None, :] // SSM_HEAD_DIM)
    return (jnp.asarray(tril, BF16), jnp.asarray(expand, BF16), jnp.asarray(bdmask, BF16))


def _mixer(x2d, batch, seq, gain, w_in, conv_a, short_w_out, conv_b, conv_b_bias, dt_bias, a_log,
           d_skip, ssm_norm, ssm_w_out, w_out, *, tm=256):
    d = D_MODEL
    ns = seq // tm
    w_dt_start = W_XBC_START + D_XBC
    w_g_start = w_dt_start + SSM_HEADS
    wmain = w_in[:, :w_dt_start].astype(BF16)
    wdt = jnp.tile(w_in[:, w_dt_start:w_g_start], (1, HEAD_COPIES)).astype(BF16)
    wg = w_in[:, w_g_start:].astype(BF16)
    dtb = jnp.tile(dt_bias, HEAD_COPIES).reshape(1, LANES)
    alog = jnp.tile(a_log, HEAD_COPIES).reshape(1, LANES)
    dexp = jnp.repeat(d_skip, SSM_HEAD_DIM).reshape(1, D_INNER)
    tril, expand, bdmask = _mixer_constants(tm)
    operands = [
        gain.reshape(1, d), wmain, wdt, wg,
        conv_a, short_w_out.astype(BF16), conv_b, conv_b_bias.reshape(1, D_XBC),
        dtb, alog, dexp,
        ssm_norm.reshape(1, D_INNER), ssm_w_out.astype(BF16), w_out.astype(BF16),
        tril, expand, bdmask,
    ]
    x_spec = pl.BlockSpec((tm, d), lambda b, s: (b * ns + s, 0))
    scratch = [
        pltpu.VMEM((tm, d), BF16),
        pltpu.VMEM((N_A_BLOCKS, SUBLANES + tm, A_BLOCK), F32),
        pltpu.VMEM((SSM_GROUPS, SUBLANES + tm, GROUP_XBC), F32),
        pltpu.VMEM((SSM_GROUPS, tm, GROUP_WIDTH), F32),
        pltpu.VMEM((SSM_GROUPS, D_STATE, GROUP_WIDTH), F32),
        pltpu.VMEM((tm // LANES, LANES, LANES), F32),
        pltpu.VMEM((N_A_BLOCKS, tm, A_BLOCK), BF16),
        pltpu.VMEM((SSM_GROUPS, tm, GROUP_WIDTH), BF16),
        pltpu.VMEM((N_A_BLOCKS, tm, A_BLOCK), BF16),
        pltpu.VMEM((N_A_BLOCKS, tm, 2 * A_BLOCK), F32),
        pltpu.VMEM((tm, d), F32),
        pltpu.VMEM((tm, d), F32),
    ]
    return pl.pallas_call(
        functools.partial(_mixer_body, tm=tm),
        grid=(batch, ns),
        in_specs=[x_spec] + [_resident(op.shape) for op in operands],
        out_specs=x_spec,
        out_shape=jax.ShapeDtypeStruct(x2d.shape, F32),
        scratch_shapes=scratch,
        compiler_params=pltpu.CompilerParams(
            dimension_semantics=("arbitrary", "arbitrary"), vmem_limit_bytes=VMEM_LIMIT_BYTES),
        name="mixer",
    )(x2d, *operands)


def kernel(x, ffn1_norm, ffn1_w_in, ffn1_w_out, mix_norm, w_in, short_conv_w, short_w_out, ssm_conv_w, ssm_conv_b, ssm_dt_bias, ssm_A_log, ssm_D, ssm_norm, ssm_w_out, w_out, ffn2_norm, ffn2_w_in, ffn2_w_out, final_norm):
    b, s, d = x.shape
    x2d = x.reshape(b * s, d)
    fin = final_norm.reshape(1, d)
    for l in range(ffn1_norm.shape[0]):
        x2d = _ffn(x2d, ffn1_norm[l].reshape(1, d), ffn1_w_in[l].astype(BF16),
                   ffn1_w_out[l].astype(BF16), fin, final_norm=False)
        x2d = _mixer(x2d, b, s, mix_norm[l], w_in[l], short_conv_w[l], short_w_out[l],
                     ssm_conv_w[l], ssm_conv_b[l], ssm_dt_bias[l], ssm_A_log[l], ssm_D[l],
                     ssm_norm[l], ssm_w_out[l], w_out[l])
        last = l == ffn1_norm.shape[0] - 1
        x2d = _ffn(x2d, ffn2_norm[l].reshape(1, d), ffn2_w_in[l].astype(BF16),
                   ffn2_w_out[l].astype(BF16), fin, final_norm=last)
    return x2d.reshape(b, s, d)
```

```python
import functools

import jax
import jax.numpy as jnp
import numpy as np
from jax import lax
from jax.experimental import pallas as pl
from jax.experimental.pallas import tpu as pltpu

D_MODEL = 1024
D_FF = 2816
D_SHORT = D_MODEL
SHORT_CONV = 3
D_INNER = 2048
SSM_HEADS = 32
SSM_HEAD_DIM = 64
SSM_GROUPS = 8
D_STATE = 128
SSM_CONV = 4
D_XBC = D_INNER + 2 * SSM_GROUPS * D_STATE
CHUNK = 64
NORM_EPS = 1e-5

HEADS_PER_GROUP = SSM_HEADS // SSM_GROUPS
GROUP_WIDTH = HEADS_PER_GROUP * SSM_HEAD_DIM
GROUP_XBC = GROUP_WIDTH + 2 * D_STATE
LANES = 128
SUBLANES = 8
HEAD_COPIES = LANES // SSM_HEADS
A_BLOCK = 256
N_A_BLOCKS = D_SHORT // A_BLOCK
GROUP_STAGGER = 3
GROUPS_PROJECTED_AHEAD = 2
YB_GROUPS = 2
W_Z_START = 3 * D_SHORT
W_XBC_START = W_Z_START + D_INNER

F32 = jnp.float32
BF16 = jnp.bfloat16

VMEM_LIMIT_BYTES = 58 * 1024 * 1024
MASKED = -1e30
LOG2_E = 1.4426950408889634


def _rms_norm(x, gain):
    ms = jnp.mean(x * x, axis=-1, keepdims=True)
    return x * lax.rsqrt(ms + NORM_EPS) * gain


def _sigmoid(x):
    return 1.0 / (1.0 + jnp.exp(-x))


def _dot(a, b):
    return jnp.dot(a, b, preferred_element_type=F32)


def _split2(x):
    hi = x.astype(BF16).astype(F32)
    lo = (x - hi).astype(BF16).astype(F32)
    return hi, lo


def _resident(shape):
    nd = len(shape)
    return pl.BlockSpec(shape, lambda *_: (0,) * nd, pipeline_mode=pl.Buffered(1))


def _ffn_body(x_ref, gain_ref, w_in_ref, w_out_ref, fin_ref, o_ref, *, ff_chunk, final_norm):
    x = x_ref[...]
    h = _rms_norm(x, gain_ref[...]).astype(BF16)
    acc = jnp.zeros(x.shape, F32)
    for j in range(D_FF // ff_chunk):
        lo = j * ff_chunk
        g = _dot(h, w_in_ref[:, lo:lo + ff_chunk])
        u = _dot(h, w_in_ref[:, D_FF + lo:D_FF + lo + ff_chunk])
        a = (g * _sigmoid(g) * u).astype(BF16)
        acc = acc + _dot(a, w_out_ref[lo:lo + ff_chunk, :])
    y = x + 0.5 * acc
    if final_norm:
        y = _rms_norm(y, fin_ref[...])
    o_ref[...] = y


def _ffn(x2d, gain, w_in, w_out, fin, *, final_norm, tm=1024, ff_chunk=256):
    t, d = x2d.shape
    body = functools.partial(_ffn_body, ff_chunk=ff_chunk, final_norm=final_norm)
    return pl.pallas_call(
        body,
        grid=(t // tm,),
        in_specs=[
            pl.BlockSpec((tm, d), lambda i: (i, 0)),
            _resident((1, d)),
            _resident(w_in.shape),
            _resident(w_out.shape),
            _resident((1, d)),
        ],
        out_specs=pl.BlockSpec((tm, d), lambda i: (i, 0)),
        out_shape=jax.ShapeDtypeStruct((t, d), F32),
        compiler_params=pltpu.CompilerParams(
            dimension_semantics=("arbitrary",), vmem_limit_bytes=VMEM_LIMIT_BYTES),
        name="ffn_final" if final_norm else "ffn",
    )(x2d, gain, w_in, w_out, fin)


def _mixer_body(x_ref, gain_ref, wmain_ref, wdt_ref, wg_ref,
                cwa_ref, swo_ref, cwb_ref, cbb_ref, dtb_ref, alog_ref, dexp_ref,
                snorm_ref, sswo_ref, wo_ref, tril_ref, expand_ref, bdmask_ref,
                o_ref,
                hb_ref, ubuf, xbuf, zbuf, st_ref, cst_ref, yap_ref, yn_ref, mg_ref, pgbuf, yabuf, ybbuf,
                *, tm):
    n_chunks = tm // CHUNK

    @pl.when(pl.program_id(1) == 0)
    def _():
        ubuf[:, 0:SUBLANES, :] = jnp.zeros((N_A_BLOCKS, SUBLANES, A_BLOCK), F32)
        xbuf[:, 0:SUBLANES, :] = jnp.zeros((SSM_GROUPS, SUBLANES, GROUP_XBC), F32)
        st_ref[...] = jnp.zeros(st_ref.shape, F32)

    x = x_ref[...]
    hb_ref[...] = _rms_norm(x, gain_ref[...]).astype(BF16)

    def project(w_ref, pieces):
        w = [w_ref[:, lo:lo + width] for lo, width in pieces]
        return _dot(hb_ref[...], w[0] if len(w) == 1 else jnp.concatenate(w, axis=1))

    def xbc_pieces(g):
        return [(g * GROUP_WIDTH, GROUP_WIDTH),
                (D_INNER + g * D_STATE, D_STATE),
                (D_INNER + (SSM_GROUPS + g) * D_STATE, D_STATE)]

    def gather_xbc(ref, rows, g):
        return jnp.concatenate([ref[rows, lo:lo + width] for lo, width in xbc_pieces(g)], axis=1)

    pre = _dot(hb_ref[...], wdt_ref[...]) + dtb_ref[...]
    dt = jnp.maximum(pre, 0.0) + jnp.log1p(jnp.exp(-jnp.abs(pre)))
    a = dt * (-jnp.exp(alog_ref[...]))
    a_hi = a.astype(BF16)
    a_r1 = a - a_hi.astype(F32)
    a_mid = a_r1.astype(BF16)
    a_lo = (a_r1 - a_mid.astype(F32)).astype(BF16)
    tril = tril_ref[...]
    cs = (_dot(tril, a_hi) + _dot(tril, a_mid) + _dot(tril, a_lo)) * LOG2_E
    cs_t = cs.T
    for blk in range(tm // LANES):
        cst_ref[blk] = cs_t[:, blk * LANES:(blk + 1) * LANES]
    cs_hi, cs_lo = _split2(cs)
    dt_hi, dt_lo = _split2(dt)
    lane_t = lax.broadcasted_iota(jnp.int32, (tm, LANES), 1)
    packed = jnp.where(lane_t < SSM_HEADS, cs_hi,
                       jnp.where(lane_t < 2 * SSM_HEADS, cs_lo,
                                 jnp.where(lane_t < 3 * SSM_HEADS, dt_hi, dt_lo))).astype(BF16)

    lane_8 = lax.broadcasted_iota(jnp.int32, (SSM_GROUPS, LANES), 1)
    cs_rows = []
    for c in range(n_chunks):
        blk = c // 2
        pieces = [cst_ref[blk, pl.ds(hh, SSM_GROUPS, stride=HEADS_PER_GROUP), :]
                  for hh in range(HEADS_PER_GROUP)]
        rolled = [pltpu.roll(p, CHUNK, axis=1) for p in pieces]
        if c % 2 == 0:
            halves = [jnp.where(lane_8 < CHUNK, pieces[0], rolled[1]),
                      jnp.where(lane_8 < CHUNK, pieces[2], rolled[3])]
        else:
            halves = [jnp.where(lane_8 < CHUNK, rolled[0], pieces[1]),
                      jnp.where(lane_8 < CHUNK, rolled[2], pieces[3])]
        cs_rows.append(jnp.concatenate(halves, axis=1))

    a_proj = {}

    def project_a(j):
        a_proj[j] = project(wmain_ref, [(part * D_SHORT + j * A_BLOCK, A_BLOCK) for part in range(3)])

    def project_xbc(g):
        xbuf[g, SUBLANES:SUBLANES + tm, :] = project(
            wmain_ref, [(W_XBC_START + lo, width) for lo, width in xbc_pieces(g)])

    def project_z(g):
        zbuf[g] = project(wmain_ref, [(W_Z_START + g * GROUP_WIDTH, GROUP_WIDTH)])

    def conv_a(j):
        cols = slice(j * A_BLOCK, (j + 1) * A_BLOCK)
        p = a_proj.pop(j)
        u = p[:, A_BLOCK:2 * A_BLOCK] * p[:, 2 * A_BLOCK:]
        ubuf[j, SUBLANES:SUBLANES + tm, :] = u
        va = cwa_ref[SHORT_CONV - 1:SHORT_CONV, cols] * u
        for k in range(SHORT_CONV - 1):
            off = SUBLANES - (SHORT_CONV - 1) + k
            va = va + cwa_ref[k:k + 1, cols] * ubuf[j, off:off + tm, :]
        ubuf[j, 0:SUBLANES, :] = ubuf[j, tm:tm + SUBLANES, :]
        yap_ref[j] = (p[:, :A_BLOCK] * va).astype(BF16)

    lane_sq = lax.broadcasted_iota(jnp.int32, (LANES, LANES), 1)
    row_i = lax.broadcasted_iota(jnp.int32, (CHUNK, GROUP_WIDTH), 0)
    lane_j = lax.broadcasted_iota(jnp.int32, (CHUNK, GROUP_WIDTH), 1) % CHUNK
    causal = row_i >= lane_j
    bdmask = bdmask_ref[...]

    def group_stages(g):
        gcols = slice(g * GROUP_WIDTH, (g + 1) * GROUP_WIDTH)
        acc = (gather_xbc(cbb_ref, slice(None), g)
               + gather_xbc(cwb_ref, slice(SSM_CONV - 1, SSM_CONV), g) * xbuf[g, SUBLANES:SUBLANES + tm, :])
        for k in range(SSM_CONV - 1):
            off = SUBLANES - (SSM_CONV - 1) + k
            acc = acc + gather_xbc(cwb_ref, slice(k, k + 1), g) * xbuf[g, off:off + tm, :]
        xbuf[g, 0:SUBLANES, :] = xbuf[g, tm:tm + SUBLANES, :]
        v = acc * _sigmoid(acc)
        yield
        xs = v[:, :GROUP_WIDTH]
        cmg = v[:, GROUP_WIDTH + D_STATE:].astype(BF16)
        bm_t = v[:, GROUP_WIDTH:GROUP_WIDTH + D_STATE].T
        bt = []
        for blk in range(tm // LANES):
            w = bm_t[:, blk * LANES:(blk + 1) * LANES]
            r = pltpu.roll(w, CHUNK, axis=1)
            bt.append(jnp.where(lane_sq < CHUNK, w, r).astype(BF16))
            bt.append(jnp.where(lane_sq < CHUNK, r, w).astype(BF16))

        ex = _dot(packed, expand_ref[:, 2 * g * GROUP_WIDTH:2 * (g + 1) * GROUP_WIDTH])
        cscol = ex[:, :GROUP_WIDTH]
        xdt = xs * ex[:, GROUP_WIDTH:]
        yield

        state = st_ref[g]
        ys = []
        for c in range(n_chunks):
            rows = slice(c * CHUNK, (c + 1) * CHUNK)
            col = cscol[rows]
            xdt_c = xdt[rows]
            cm_c = cmg[rows]
            cb4 = _dot(cm_c, jnp.concatenate([bt[c], bt[c]], axis=1))
            decay = jnp.exp2(jnp.where(causal, col - cs_rows[c][g:g + 1, :], MASKED))
            m = (cb4 * decay).astype(BF16)
            xd = xdt_c.astype(BF16)
            bd = jnp.concatenate([xd] * HEADS_PER_GROUP, axis=0) * bdmask
            y_off = _dot(cm_c, state.astype(BF16)) * jnp.exp2(col)
            ys.append(_dot(m, bd) + y_off + xs[rows] * dexp_ref[:, gcols])
            last = col[CHUNK - 1:CHUNK, :]
            xp = (xdt_c * jnp.exp2(last - col)).astype(BF16)
            state = state * jnp.exp2(last) + _dot(bt[c], jnp.concatenate([xp, jnp.zeros_like(xp)], axis=0))
            yield
        st_ref[g] = state
        y = jnp.concatenate(ys, axis=0)

        z = zbuf[g]
        yz = y * (z * _sigmoid(z))
        yn = yz * lax.rsqrt(jnp.mean(yz * yz, axis=-1, keepdims=True) + NORM_EPS) * snorm_ref[:, gcols]
        yn_ref[g] = yn.astype(BF16)

    def ssd_out_partial(g_lo, g_hi):
        yn = jnp.concatenate([yn_ref[g] for g in range(g_lo, g_hi)], axis=1)
        return _dot(yn, sswo_ref[g_lo * GROUP_WIDTH:g_hi * GROUP_WIDTH, :])

    def project_gates(j):
        pgbuf[j] = project(wg_ref, [(part * D_MODEL + j * A_BLOCK, A_BLOCK) for part in range(2)])

    def project_ya():
        yabuf[...] = _dot(jnp.concatenate([yap_ref[j] for j in range(N_A_BLOCKS)], axis=1), swo_ref[...])

    def accumulate_yb(g_lo, g_hi):
        if g_lo == 0:
            ybbuf[...] = ssd_out_partial(g_lo, g_hi)
        else:
            ybbuf[...] += ssd_out_partial(g_lo, g_hi)

    fillers = []
    for j in range(N_A_BLOCKS):
        fillers += [functools.partial(project_a, j), functools.partial(conv_a, j)]
    fillers += [project_ya] + [functools.partial(project_gates, j) for j in range(N_A_BLOCKS)]
    for g in range(GROUPS_PROJECTED_AHEAD):
        project_xbc(g)
        project_z(g)
    dense = []
    for g in range(GROUPS_PROJECTED_AHEAD, SSM_GROUPS):
        dense.append((0, g, functools.partial(project_xbc, g)))
        dense.append((0, g, functools.partial(project_z, g)))
        if fillers:
            dense.append((0, None, fillers.pop(0)))
    dense += [(0, None, f) for f in fillers]
    for g in range(0, SSM_GROUPS, YB_GROUPS):
        dense.append((g + YB_GROUPS, None, functools.partial(accumulate_yb, g, g + YB_GROUPS)))

    live, started, finished, rounds = [], 0, 0, 0
    while started < SSM_GROUPS or live:
        if started < SSM_GROUPS and rounds % GROUP_STAGGER == 0:
            for entry in [e for e in dense if e[1] == started]:
                entry[2]()
                dense.remove(entry)
            live.append(group_stages(started))
            started += 1
        still_live = [stages for stages in live if next(stages, True) is None]
        finished += len(live) - len(still_live)
        live = still_live
        ready = [e for e in dense if e[0] <= finished]
        if ready:
            ready[0][2]()
            dense.remove(ready[0])
        rounds += 1
    for entry in dense:
        entry[2]()

    for j in range(N_A_BLOCKS):
        cols = slice(j * A_BLOCK, (j + 1) * A_BLOCK)
        pg = pgbuf[j]
        merged = _sigmoid(pg[:, :A_BLOCK]) * yabuf[:, cols] + _sigmoid(pg[:, A_BLOCK:]) * ybbuf[:, cols]
        mg_ref[j] = merged.astype(BF16)
    o_ref[...] = x + _dot(jnp.concatenate([mg_ref[j] for j in range(N_A_BLOCKS)], axis=1), wo_ref[...])


def _mixer_constants(tm):
    t = np.arange(tm)
    tril = ((t[:, None] >= t[None, :]) & (t[:, None] // CHUNK == t[None, :] // CHUNK))
    k = np.arange(LANES)[:, None]
    col = np.arange(2 * D_INNER)[None, :]
    group, within = col // (2 * GROUP_WIDTH), col % (2 * GROUP_WIDTH)
    head = group * HEADS_PER_GROUP + (within % GROUP_WIDTH) // SSM_HEAD_DIM
    half = within // GROUP_WIDTH
    expand = (k % SSM_HEADS == head) & ((k // (2 * SSM_HEADS)) == half)
    r = np.arange(GROUP_WIDTH)
    bdmask = (r[:, None] // SSM_HEAD_DIM) == (r[None, :] // SSM_HEAD_DIM)
    return (jnp.asarray(tril, BF16), jnp.asarray(expand, BF16), jnp.asarray(bdmask, BF16))


def _mixer(x2d, batch, seq, gain, w_in, conv_a, short_w_out, conv_b, conv_b_bias, dt_bias, a_log,
           d_skip, ssm_norm, ssm_w_out, w_out, *, tm=256):
    d = D_MODEL
    ns = seq // tm
    w_dt_start = W_XBC_START + D_XBC
    w_g_start = w_dt_start + SSM_HEADS
    wmain = w_in[:, :w_dt_start].astype(BF16)
    wdt = jnp.tile(w_in[:, w_dt_start:w_g_start], (1, HEAD_COPIES)).astype(BF16)
    wg = w_in[:, w_g_start:].astype(BF16)
    dtb = jnp.tile(dt_bias, HEAD_COPIES).reshape(1, LANES)
    alog = jnp.tile(a_log, HEAD_COPIES).reshape(1, LANES)
    dexp = jnp.repeat(d_skip, SSM_HEAD_DIM).reshape(1, D_INNER)
    tril, expand, bdmask = _mixer_constants(tm)
    operands = [
        gain.reshape(1, d), wmain, wdt, wg,
        conv_a, short_w_out.astype(BF16), conv_b, conv_b_bias.reshape(1, D_XBC),
        dtb, alog, dexp,
        ssm_norm.reshape(1, D_INNER), ssm_w_out.astype(BF16), w_out.astype(BF16),
        tril, expand, bdmask,
    ]
    x_spec = pl.BlockSpec((tm, d), lambda b, s: (b * ns + s, 0))
    scratch = [
        pltpu.VMEM((tm, d), BF16),
        pltpu.VMEM((N_A_BLOCKS, SUBLANES + tm, A_BLOCK), F32),
        pltpu.VMEM((SSM_GROUPS, SUBLANES + tm, GROUP_XBC), F32),
        pltpu.VMEM((SSM_GROUPS, tm, GROUP_WIDTH), F32),
        pltpu.VMEM((SSM_GROUPS, D_STATE, GROUP_WIDTH), F32),
        pltpu.VMEM((tm // LANES, LANES, LANES), F32),
        pltpu.VMEM((N_A_BLOCKS, tm, A_BLOCK), BF16),
        pltpu.VMEM((SSM_GROUPS, tm, GROUP_WIDTH), BF16),
        pltpu.VMEM((N_A_BLOCKS, tm, A_BLOCK), BF16),
        pltpu.VMEM((N_A_BLOCKS, tm, 2 * A_BLOCK), F32),
        pltpu.VMEM((tm, d), F32),
        pltpu.VMEM((tm, d), F32),
    ]
    return pl.pallas_call(
        functools.partial(_mixer_body, tm=tm),
        grid=(batch, ns),
        in_specs=[x_spec] + [_resident(op.shape) for op in operands],
        out_specs=x_spec,
        out_shape=jax.ShapeDtypeStruct(x2d.shape, F32),
        scratch_shapes=scratch,
        compiler_params=pltpu.CompilerParams(
            dimension_semantics=("arbitrary", "arbitrary"), vmem_limit_bytes=VMEM_LIMIT_BYTES),
        name="mixer",
    )(x2d, *operands)


def kernel(x, ffn1_norm, ffn1_w_in, ffn1_w_out, mix_norm, w_in, short_conv_w, short_w_out, ssm_conv_w, ssm_conv_b, ssm_dt_bias, ssm_A_log, ssm_D, ssm_norm, ssm_w_out, w_out, ffn2_norm, ffn2_w_in, ffn2_w_out, final_norm):
    b, s, d = x.shape
    x2d = x.reshape(b * s, d)
    fin = final_norm.reshape(1, d)
    for l in range(ffn1_norm.shape[0]):
        x2d = _ffn(x2d, ffn1_norm[l].reshape(1, d), ffn1_w_in[l].astype(BF16),
                   ffn1_w_out[l].astype(BF16), fin, final_norm=False)
        x2d = _mixer(x2d, b, s, mix_norm[l], w_in[l], short_conv_w[l], short_w_out[l],
                     ssm_conv_w[l], ssm_conv_b[l], ssm_dt_bias[l], ssm_A_log[l], ssm_D[l],
                     ssm_norm[l], ssm_w_out[l], w_out[l])
        last = l == ffn1_norm.shape[0] - 1
        x2d = _ffn(x2d, ffn2_norm[l].reshape(1, d), ffn2_w_in[l].astype(BF16),
                   ffn2_w_out[l].astype(BF16), fin, final_norm=last)
    return x2d.reshape(b, s, d)
```

```python
import functools

import jax
import jax.numpy as jnp
import numpy as np
from jax import lax
from jax.experimental import pallas as pl
from jax.experimental.pallas import tpu as pltpu

D_MODEL = 1024
D_FF = 2816
D_SHORT = D_MODEL
SHORT_CONV = 3
D_INNER = 2048
SSM_HEADS = 32
SSM_HEAD_DIM = 64
SSM_GROUPS = 8
D_STATE = 128
SSM_CONV = 4
D_XBC = D_INNER + 2 * SSM_GROUPS * D_STATE
CHUNK = 64
NORM_EPS = 1e-5

HEADS_PER_GROUP = SSM_HEADS // SSM_GROUPS
GROUP_WIDTH = HEADS_PER_GROUP * SSM_HEAD_DIM
GROUP_XBC = GROUP_WIDTH + 2 * D_STATE
LANES = 128
SUBLANES = 8
HEAD_COPIES = LANES // SSM_HEADS
A_BLOCK = 256
N_A_BLOCKS = D_SHORT // A_BLOCK
GROUP_STAGGER = 3
GROUPS_PROJECTED_AHEAD = 2
YB_GROUPS = 2
CAST_BLOCK = 512
W_Z_START = 3 * D_SHORT
W_XBC_START = W_Z_START + D_INNER

F32 = jnp.float32
BF16 = jnp.bfloat16

VMEM_LIMIT_BYTES = 58 * 1024 * 1024
MASKED = -1e30
LOG2_E = 1.4426950408889634


def _rms_norm(x, gain):
    ms = jnp.mean(x * x, axis=-1, keepdims=True)
    return x * lax.rsqrt(ms + NORM_EPS) * gain


def _sigmoid(x):
    return 1.0 / (1.0 + jnp.exp(-x))


def _dot(a, b):
    return jnp.dot(a, b, preferred_element_type=F32)


def _split2(x):
    hi = x.astype(BF16).astype(F32)
    lo = (x - hi).astype(BF16).astype(F32)
    return hi, lo


def _resident(shape):
    nd = len(shape)
    return pl.BlockSpec(shape, lambda *_: (0,) * nd, pipeline_mode=pl.Buffered(1))


def _ffn_body(x_ref, gain_ref, w_in_ref, w_out_ref, fin_ref, o_ref, *, ff_chunk, final_norm):
    x = x_ref[...]
    h = _rms_norm(x, gain_ref[...]).astype(BF16)
    acc = jnp.zeros(x.shape, F32)
    for j in range(D_FF // ff_chunk):
        lo = j * ff_chunk
        g = _dot(h, w_in_ref[:, lo:lo + ff_chunk])
        u = _dot(h, w_in_ref[:, D_FF + lo:D_FF + lo + ff_chunk])
        a = (g * _sigmoid(g) * u).astype(BF16)
        acc = acc + _dot(a, w_out_ref[lo:lo + ff_chunk, :])
    y = x + 0.5 * acc
    if final_norm:
        y = _rms_norm(y, fin_ref[...])
    o_ref[...] = y


def _ffn(x2d, gain, w_in, w_out, fin, *, final_norm, tm=1024, ff_chunk=256):
    t, d = x2d.shape
    body = functools.partial(_ffn_body, ff_chunk=ff_chunk, final_norm=final_norm)
    return pl.pallas_call(
        body,
        grid=(t // tm,),
        in_specs=[
            pl.BlockSpec((tm, d), lambda i: (i, 0)),
            _resident((1, d)),
            _resident(w_in.shape),
            _resident(w_out.shape),
            _resident((1, d)),
        ],
        out_specs=pl.BlockSpec((tm, d), lambda i: (i, 0)),
        out_shape=jax.ShapeDtypeStruct((t, d), F32),
        compiler_params=pltpu.CompilerParams(
            dimension_semantics=("arbitrary",), vmem_limit_bytes=VMEM_LIMIT_BYTES),
        name="ffn_final" if final_norm else "ffn",
    )(x2d, gain, w_in, w_out, fin)


def _mixer_body(x_ref, gain_ref, wmain_ref, wdt_ref, wg_ref,
                cwa_ref, swo_ref, cwb_ref, cbb_ref, dtb_ref, alog_ref, dexp_ref,
                snorm_ref, sswo_ref, wo_ref, tril_ref, expand_ref, bdmask_ref,
                o_ref,
                hb_ref, ubuf, xbuf, zbuf, st_ref, cst_ref, yap_ref, yn_ref, mg_ref, pgbuf, yabuf, ybbuf,
                *, tm):
    n_chunks = tm // CHUNK

    @pl.when(pl.program_id(1) == 0)
    def _():
        ubuf[:, 0:SUBLANES, :] = jnp.zeros((N_A_BLOCKS, SUBLANES, A_BLOCK), F32)
        xbuf[:, 0:SUBLANES, :] = jnp.zeros((SSM_GROUPS, SUBLANES, GROUP_XBC), F32)
        st_ref[...] = jnp.zeros(st_ref.shape, F32)

    x = x_ref[...]
    hb_ref[...] = _rms_norm(x, gain_ref[...]).astype(BF16)

    def project(w_ref, pieces):
        w = [w_ref[:, lo:lo + width] for lo, width in pieces]
        return _dot(hb_ref[...], w[0] if len(w) == 1 else jnp.concatenate(w, axis=1))

    def xbc_pieces(g):
        return [(g * GROUP_WIDTH, GROUP_WIDTH),
                (D_INNER + g * D_STATE, D_STATE),
                (D_INNER + (SSM_GROUPS + g) * D_STATE, D_STATE)]

    def gather_xbc(ref, rows, g):
        return jnp.concatenate([ref[rows, lo:lo + width] for lo, width in xbc_pieces(g)], axis=1)

    pre = _dot(hb_ref[...], wdt_ref[...]) + dtb_ref[...]
    dt = jnp.maximum(pre, 0.0) + jnp.log1p(jnp.exp(-jnp.abs(pre)))
    a = dt * (-jnp.exp(alog_ref[...]))
    a_hi = a.astype(BF16)
    a_r1 = a - a_hi.astype(F32)
    a_mid = a_r1.astype(BF16)
    a_lo = (a_r1 - a_mid.astype(F32)).astype(BF16)
    tril = tril_ref[...]
    cs = (_dot(tril, a_hi) + _dot(tril, a_mid) + _dot(tril, a_lo)) * LOG2_E
    cs_t = cs.T
    for blk in range(tm // LANES):
        cst_ref[blk] = cs_t[:, blk * LANES:(blk + 1) * LANES]
    cs_hi, cs_lo = _split2(cs)
    dt_hi, dt_lo = _split2(dt)
    lane_t = lax.broadcasted_iota(jnp.int32, (tm, LANES), 1)
    packed = jnp.where(lane_t < SSM_HEADS, cs_hi,
                       jnp.where(lane_t < 2 * SSM_HEADS, cs_lo,
                                 jnp.where(lane_t < 3 * SSM_HEADS, dt_hi, dt_lo))).astype(BF16)

    lane_8 = lax.broadcasted_iota(jnp.int32, (SSM_GROUPS, LANES), 1)
    cs_rows = []
    for c in range(n_chunks):
        blk = c // 2
        pieces = [cst_ref[blk, pl.ds(hh, SSM_GROUPS, stride=HEADS_PER_GROUP), :]
                  for hh in range(HEADS_PER_GROUP)]
        rolled = [pltpu.roll(p, CHUNK, axis=1) for p in pieces]
        if c % 2 == 0:
            halves = [jnp.where(lane_8 < CHUNK, pieces[0], rolled[1]),
                      jnp.where(lane_8 < CHUNK, pieces[2], rolled[3])]
        else:
            halves = [jnp.where(lane_8 < CHUNK, rolled[0], pieces[1]),
                      jnp.where(lane_8 < CHUNK, rolled[2], pieces[3])]
        cs_rows.append(jnp.concatenate(halves, axis=1))

    a_proj = {}

    def project_a(j):
        a_proj[j] = project(wmain_ref, [(part * D_SHORT + j * A_BLOCK, A_BLOCK) for part in range(3)])

    def project_xbc(g):
        xbuf[g, SUBLANES:SUBLANES + tm, :] = project(
            wmain_ref, [(W_XBC_START + lo, width) for lo, width in xbc_pieces(g)])

    def project_z(g):
        zbuf[g] = project(wmain_ref, [(W_Z_START + g * GROUP_WIDTH, GROUP_WIDTH)])

    def conv_a(j):
        cols = slice(j * A_BLOCK, (j + 1) * A_BLOCK)
        p = a_proj.pop(j)
        u = p[:, A_BLOCK:2 * A_BLOCK] * p[:, 2 * A_BLOCK:]
        ubuf[j, SUBLANES:SUBLANES + tm, :] = u
        va = cwa_ref[SHORT_CONV - 1:SHORT_CONV, cols] * u
        for k in range(SHORT_CONV - 1):
            off = SUBLANES - (SHORT_CONV - 1) + k
            va = va + cwa_ref[k:k + 1, cols] * ubuf[j, off:off + tm, :]
        ubuf[j, 0:SUBLANES, :] = ubuf[j, tm:tm + SUBLANES, :]
        yap_ref[j] = (p[:, :A_BLOCK] * va).astype(BF16)

    lane_sq = lax.broadcasted_iota(jnp.int32, (LANES, LANES), 1)
    row_i = lax.broadcasted_iota(jnp.int32, (CHUNK, GROUP_WIDTH), 0)
    lane_j = lax.broadcasted_iota(jnp.int32, (CHUNK, GROUP_WIDTH), 1) % CHUNK
    causal = row_i >= lane_j
    bdmask = bdmask_ref[...]

    def group_stages(g):
        gcols = slice(g * GROUP_WIDTH, (g + 1) * GROUP_WIDTH)
        acc = (gather_xbc(cbb_ref, slice(None), g)
               + gather_xbc(cwb_ref, slice(SSM_CONV - 1, SSM_CONV), g) * xbuf[g, SUBLANES:SUBLANES + tm, :])
        for k in range(SSM_CONV - 1):
            off = SUBLANES - (SSM_CONV - 1) + k
            acc = acc + gather_xbc(cwb_ref, slice(k, k + 1), g) * xbuf[g, off:off + tm, :]
        xbuf[g, 0:SUBLANES, :] = xbuf[g, tm:tm + SUBLANES, :]
        v = acc * _sigmoid(acc)
        yield
        xs = v[:, :GROUP_WIDTH]
        cmg = v[:, GROUP_WIDTH + D_STATE:].astype(BF16)
        bm_t = v[:, GROUP_WIDTH:GROUP_WIDTH + D_STATE].T
        bt = []
        for blk in range(tm // LANES):
            w = bm_t[:, blk * LANES:(blk + 1) * LANES]
            r = pltpu.roll(w, CHUNK, axis=1)
            bt.append(jnp.where(lane_sq < CHUNK, w, r).astype(BF16))
            bt.append(jnp.where(lane_sq < CHUNK, r, w).astype(BF16))

        ex = _dot(packed, expand_ref[:, 2 * g * GROUP_WIDTH:2 * (g + 1) * GROUP_WIDTH])
        cscol = ex[:, :GROUP_WIDTH]
        xdt = xs * ex[:, GROUP_WIDTH:]
        yield

        state = st_ref[g]
        ys = []
        for c in range(n_chunks):
            rows = slice(c * CHUNK, (c + 1) * CHUNK)
            col = cscol[rows]
            xdt_c = xdt[rows]
            cm_c = cmg[rows]
            cb4 = _dot(cm_c, jnp.concatenate([bt[c], bt[c]], axis=1))
            decay = jnp.exp2(jnp.where(causal, col - cs_rows[c][g:g + 1, :], MASKED))
            m = (cb4 * decay).astype(BF16)
            xd = xdt_c.astype(BF16)
            bd = jnp.concatenate([xd] * HEADS_PER_GROUP, axis=0) * bdmask
            y_off = _dot(cm_c, state.astype(BF16)) * jnp.exp2(col)
            ys.append(_dot(m, bd) + y_off + xs[rows] * dexp_ref[:, gcols])
            last = col[CHUNK - 1:CHUNK, :]
            xp = (xdt_c * jnp.exp2(last - col)).astype(BF16)
            state = state * jnp.exp2(last) + _dot(bt[c], jnp.concatenate([xp, jnp.zeros_like(xp)], axis=0))
            yield
        st_ref[g] = state
        y = jnp.concatenate(ys, axis=0)

        z = zbuf[g]
        yz = y * (z * _sigmoid(z))
        yn = yz * lax.rsqrt(jnp.mean(yz * yz, axis=-1, keepdims=True) + NORM_EPS) * snorm_ref[:, gcols]
        yn_ref[g] = yn.astype(BF16)

    def ssd_out_partial(g_lo, g_hi):
        yn = jnp.concatenate([yn_ref[g] for g in range(g_lo, g_hi)], axis=1)
        return _dot(yn, sswo_ref[g_lo * GROUP_WIDTH:g_hi * GROUP_WIDTH, :])

    def project_gates(j):
        pgbuf[j] = project(wg_ref, [(part * D_MODEL + j * A_BLOCK, A_BLOCK) for part in range(2)])

    def project_ya():
        yabuf[...] = _dot(jnp.concatenate([yap_ref[j] for j in range(N_A_BLOCKS)], axis=1), swo_ref[...])

    def accumulate_yb(g_lo, g_hi):
        if g_lo == 0:
            ybbuf[...] = ssd_out_partial(g_lo, g_hi)
        else:
            ybbuf[...] += ssd_out_partial(g_lo, g_hi)

    fillers = []
    for j in range(N_A_BLOCKS):
        fillers += [functools.partial(project_a, j), functools.partial(conv_a, j)]
    fillers += [project_ya] + [functools.partial(project_gates, j) for j in range(N_A_BLOCKS)]
    for g in range(GROUPS_PROJECTED_AHEAD):
        project_xbc(g)
        project_z(g)
    dense = []
    for g in range(GROUPS_PROJECTED_AHEAD, SSM_GROUPS):
        dense.append((0, g, functools.partial(project_xbc, g)))
        dense.append((0, g, functools.partial(project_z, g)))
        if fillers:
            dense.append((0, None, fillers.pop(0)))
    dense += [(0, None, f) for f in fillers]
    for g in range(0, SSM_GROUPS, YB_GROUPS):
        dense.append((g + YB_GROUPS, None, functools.partial(accumulate_yb, g, g + YB_GROUPS)))

    live, started, finished, rounds = [], 0, 0, 0
    while started < SSM_GROUPS or live:
        if started < SSM_GROUPS and rounds % GROUP_STAGGER == 0:
            for entry in [e for e in dense if e[1] == started]:
                entry[2]()
                dense.remove(entry)
            live.append(group_stages(started))
            started += 1
        still_live = [stages for stages in live if next(stages, True) is None]
        finished += len(live) - len(still_live)
        live = still_live
        ready = [e for e in dense if e[0] <= finished]
        if ready:
            ready[0][2]()
            dense.remove(ready[0])
        rounds += 1
    for entry in dense:
        entry[2]()

    for j in range(N_A_BLOCKS):
        cols = slice(j * A_BLOCK, (j + 1) * A_BLOCK)
        pg = pgbuf[j]
        merged = _sigmoid(pg[:, :A_BLOCK]) * yabuf[:, cols] + _sigmoid(pg[:, A_BLOCK:]) * ybbuf[:, cols]
        mg_ref[j] = merged.astype(BF16)
    o_ref[...] = x + _dot(jnp.concatenate([mg_ref[j] for j in range(N_A_BLOCKS)], axis=1), wo_ref[...])


def _mixer_constants(tm):
    t = np.arange(tm)
    tril = ((t[:, None] >= t[None, :]) & (t[:, None] // CHUNK == t[None, :] // CHUNK))
    k = np.arange(LANES)[:, None]
    col = np.arange(2 * D_INNER)[None, :]
    group, within = col // (2 * GROUP_WIDTH), col % (2 * GROUP_WIDTH)
    head = group * HEADS_PER_GROUP + (within % GROUP_WIDTH) // SSM_HEAD_DIM
    half = within // GROUP_WIDTH
    expand = (k % SSM_HEADS == head) & ((k // (2 * SSM_HEADS)) == half)
    r = np.arange(GROUP_WIDTH)
    bdmask = (r[:, None] // SSM_HEAD_DIM) == (r[None, :] // SSM_HEAD_DIM)
    return (jnp.asarray(tril, BF16), jnp.asarray(expand, BF16), jnp.asarray(bdmask, BF16))


def _cast_body(src_ref, dst_ref):
    dst_ref[...] = src_ref[...].astype(BF16)


def _cast_shifted_body(lo_ref, hi_ref, dst_ref, *, shift):
    dst_ref[...] = jnp.concatenate([lo_ref[:, shift:], hi_ref[:, :shift]], axis=1).astype(BF16)


def _cast_columns(w, start, width):
    rows = w.shape[0]
    first, shift = divmod(start, CAST_BLOCK)
    out_spec = pl.BlockSpec((rows, CAST_BLOCK), lambda i: (0, i))
    if shift == 0:
        body, in_specs, args = _cast_body, [pl.BlockSpec((rows, CAST_BLOCK), lambda i: (0, first + i))], (w,)
    else:
        assert start + width <= w.shape[1]
        body = functools.partial(_cast_shifted_body, shift=shift)
        in_specs = [pl.BlockSpec((rows, CAST_BLOCK), lambda i: (0, first + i)),
                    pl.BlockSpec((rows, CAST_BLOCK), lambda i: (0, first + i + 1))]
        args = (w, w)
    return pl.pallas_call(
        body, grid=(width // CAST_BLOCK,), in_specs=in_specs, out_specs=out_spec,
        out_shape=jax.ShapeDtypeStruct((rows, width), BF16),
        compiler_params=pltpu.CompilerParams(dimension_semantics=("arbitrary",)),
        name="cast_columns",
    )(*args)


def _mixer(x2d, batch, seq, gain, w_in, conv_a, short_w_out, conv_b, conv_b_bias, dt_bias, a_log,
           d_skip, ssm_norm, ssm_w_out, w_out, *, tm=256):
    d = D_MODEL
    ns = seq // tm
    w_dt_start = W_XBC_START + D_XBC
    w_g_start = w_dt_start + SSM_HEADS
    wmain = _cast_columns(w_in, 0, w_dt_start)
    wdt = jnp.tile(w_in[:, w_dt_start:w_g_start], (1, HEAD_COPIES)).astype(BF16)
    wg = _cast_columns(w_in, w_g_start, 2 * D_MODEL)
    dtb = jnp.tile(dt_bias, HEAD_COPIES).reshape(1, LANES)
    alog = jnp.tile(a_log, HEAD_COPIES).reshape(1, LANES)
    dexp = jnp.repeat(d_skip, SSM_HEAD_DIM).reshape(1, D_INNER)
    tril, expand, bdmask = _mixer_constants(tm)
    operands = [
        gain.reshape(1, d), wmain, wdt, wg,
        conv_a, short_w_out.astype(BF16), conv_b, conv_b_bias.reshape(1, D_XBC),
        dtb, alog, dexp,
        ssm_norm.reshape(1, D_INNER), ssm_w_out.astype(BF16), w_out.astype(BF16),
        tril, expand, bdmask,
    ]
    x_spec = pl.BlockSpec((tm, d), lambda b, s: (b * ns + s, 0))
    scratch = [
        pltpu.VMEM((tm, d), BF16),
        pltpu.VMEM((N_A_BLOCKS, SUBLANES + tm, A_BLOCK), F32),
        pltpu.VMEM((SSM_GROUPS, SUBLANES + tm, GROUP_XBC), F32),
        pltpu.VMEM((SSM_GROUPS, tm, GROUP_WIDTH), F32),
        pltpu.VMEM((SSM_GROUPS, D_STATE, GROUP_WIDTH), F32),
        pltpu.VMEM((tm // LANES, LANES, LANES), F32),
        pltpu.VMEM((N_A_BLOCKS, tm, A_BLOCK), BF16),
        pltpu.VMEM((SSM_GROUPS, tm, GROUP_WIDTH), BF16),
        pltpu.VMEM((N_A_BLOCKS, tm, A_BLOCK), BF16),
        pltpu.VMEM((N_A_BLOCKS, tm, 2 * A_BLOCK), F32),
        pltpu.VMEM((tm, d), F32),
        pltpu.VMEM((tm, d), F32),
    ]
    return pl.pallas_call(
        functools.partial(_mixer_body, tm=tm),
        grid=(batch, ns),
        in_specs=[x_spec] + [_resident(op.shape) for op in operands],
        out_specs=x_spec,
        out_shape=jax.ShapeDtypeStruct(x2d.shape, F32),
        scratch_shapes=scratch,
        compiler_params=pltpu.CompilerParams(
            dimension_semantics=("arbitrary", "arbitrary"), vmem_limit_bytes=VMEM_LIMIT_BYTES),
        name="mixer",
    )(x2d, *operands)


def kernel(x, ffn1_norm, ffn1_w_in, ffn1_w_out, mix_norm, w_in, short_conv_w, short_w_out, ssm_conv_w, ssm_conv_b, ssm_dt_bias, ssm_A_log, ssm_D, ssm_norm, ssm_w_out, w_out, ffn2_norm, ffn2_w_in, ffn2_w_out, final_norm):
    b, s, d = x.shape
    x2d = x.reshape(b * s, d)
    fin = final_norm.reshape(1, d)
    for l in range(ffn1_norm.shape[0]):
        x2d = _ffn(x2d, ffn1_norm[l].reshape(1, d), ffn1_w_in[l].astype(BF16),
                   ffn1_w_out[l].astype(BF16), fin, final_norm=False)
        x2d = _mixer(x2d, b, s, mix_norm[l], w_in[l], short_conv_w[l], short_w_out[l],
                     ssm_conv_w[l], ssm_conv_b[l], ssm_dt_bias[l], ssm_A_log[l], ssm_D[l],
                     ssm_norm[l], ssm_w_out[l], w_out[l])
        last = l == ffn1_norm.shape[0] - 1
        x2d = _ffn(x2d, ffn2_norm[l].reshape(1, d), ffn2_w_in[l].astype(BF16),
                   ffn2_w_out[l].astype(BF16), fin, final_norm=last)
    return x2d.reshape(b, s, d)
```

```python
import functools

import jax
import jax.numpy as jnp
import numpy as np
from jax import lax
from jax.experimental import pallas as pl
from jax.experimental.pallas import tpu as pltpu

D_MODEL = 1024
D_FF = 2816
D_SHORT = D_MODEL
SHORT_CONV = 3
D_INNER = 2048
SSM_HEADS = 32
SSM_HEAD_DIM = 64
SSM_GROUPS = 8
D_STATE = 128
SSM_CONV = 4
D_XBC = D_INNER + 2 * SSM_GROUPS * D_STATE
CHUNK = 64
NORM_EPS = 1e-5

HEADS_PER_GROUP = SSM_HEADS // SSM_GROUPS
GROUP_WIDTH = HEADS_PER_GROUP * SSM_HEAD_DIM
GROUP_XBC = GROUP_WIDTH + 2 * D_STATE
LANES = 128
SUBLANES = 8
HEAD_COPIES = LANES // SSM_HEADS
A_BLOCK = 256
N_A_BLOCKS = D_SHORT // A_BLOCK
GROUP_STAGGER = 3
GROUPS_PROJECTED_AHEAD = 2
YB_GROUPS = 2
CAST_BLOCK = 512
W_Z_START = 3 * D_SHORT
W_XBC_START = W_Z_START + D_INNER

F32 = jnp.float32
BF16 = jnp.bfloat16

VMEM_LIMIT_BYTES = 58 * 1024 * 1024
MASKED = -1e30
LOG2_E = 1.4426950408889634


def _rms_norm(x, gain):
    ms = jnp.mean(x * x, axis=-1, keepdims=True)
    return x * lax.rsqrt(ms + NORM_EPS) * gain


def _sigmoid(x):
    return 1.0 / (1.0 + jnp.exp(-x))


def _dot(a, b):
    return jnp.dot(a, b, preferred_element_type=F32)


def _split2(x):
    hi = x.astype(BF16).astype(F32)
    lo = (x - hi).astype(BF16).astype(F32)
    return hi, lo


def _resident(shape):
    nd = len(shape)
    return pl.BlockSpec(shape, lambda *_: (0,) * nd, pipeline_mode=pl.Buffered(1))


def _ffn_body(x_ref, gain_ref, w_in_ref, w_out_ref, fin_ref, o_ref, *, ff_chunk, final_norm):
    x = x_ref[...]
    h = _rms_norm(x, gain_ref[...]).astype(BF16)
    acc = jnp.zeros(x.shape, F32)
    for j in range(D_FF // ff_chunk):
        lo = j * ff_chunk
        g = _dot(h, w_in_ref[:, lo:lo + ff_chunk])
        u = _dot(h, w_in_ref[:, D_FF + lo:D_FF + lo + ff_chunk])
        a = (g * _sigmoid(g) * u).astype(BF16)
        acc = acc + _dot(a, w_out_ref[lo:lo + ff_chunk, :])
    y = x + 0.5 * acc
    if final_norm:
        y = _rms_norm(y, fin_ref[...])
    o_ref[...] = y


def _ffn(x2d, gain, w_in, w_out, fin, *, final_norm, tm=1024, ff_chunk=256):
    t, d = x2d.shape
    body = functools.partial(_ffn_body, ff_chunk=ff_chunk, final_norm=final_norm)
    return pl.pallas_call(
        body,
        grid=(t // tm,),
        in_specs=[
            pl.BlockSpec((tm, d), lambda i: (i, 0)),
            _resident((1, d)),
            _resident(w_in.shape),
            _resident(w_out.shape),
            _resident((1, d)),
        ],
        out_specs=pl.BlockSpec((tm, d), lambda i: (i, 0)),
        out_shape=jax.ShapeDtypeStruct((t, d), F32),
        compiler_params=pltpu.CompilerParams(
            dimension_semantics=("arbitrary",), vmem_limit_bytes=VMEM_LIMIT_BYTES),
        name="ffn_final" if final_norm else "ffn",
    )(x2d, gain, w_in, w_out, fin)


def _mixer_body(x_ref, gain_ref, wmain_ref, wdt_ref, wg_ref,
                cwa_ref, swo_ref, cwb_ref, cbb_ref, dtb_ref, alog_ref, dexp_ref,
                snorm_ref, sswo_ref, wo_ref, tril_ref, expand_ref, bdmask_ref,
                o_ref,
                hb_ref, ubuf, xbuf, zbuf, st_ref, cst_ref, yap_ref, yn_ref, mg_ref, pgbuf, yabuf, ybbuf,
                *, tm):
    n_chunks = tm // CHUNK

    @pl.when(pl.program_id(1) == 0)
    def _():
        ubuf[:, 0:SUBLANES, :] = jnp.zeros((N_A_BLOCKS, SUBLANES, A_BLOCK), F32)
        xbuf[:, 0:SUBLANES, :] = jnp.zeros((SSM_GROUPS, SUBLANES, GROUP_XBC), F32)
        st_ref[...] = jnp.zeros(st_ref.shape, F32)

    x = x_ref[...]
    hb_ref[...] = _rms_norm(x, gain_ref[...]).astype(BF16)

    def project(w_ref, pieces):
        w = [w_ref[:, lo:lo + width] for lo, width in pieces]
        return _dot(hb_ref[...], w[0] if len(w) == 1 else jnp.concatenate(w, axis=1))

    def xbc_pieces(g):
        return [(g * GROUP_WIDTH, GROUP_WIDTH),
                (D_INNER + g * D_STATE, D_STATE),
                (D_INNER + (SSM_GROUPS + g) * D_STATE, D_STATE)]

    def gather_xbc(ref, rows, g):
        return jnp.concatenate([ref[rows, lo:lo + width] for lo, width in xbc_pieces(g)], axis=1)

    pre = _dot(hb_ref[...], wdt_ref[...]) + dtb_ref[...]
    dt = jnp.maximum(pre, 0.0) + jnp.log1p(jnp.exp(-jnp.abs(pre)))
    a = dt * (-jnp.exp(alog_ref[...]))
    a_hi = a.astype(BF16)
    a_r1 = a - a_hi.astype(F32)
    a_mid = a_r1.astype(BF16)
    a_lo = (a_r1 - a_mid.astype(F32)).astype(BF16)
    tril = tril_ref[...]
    cs = (_dot(tril, a_hi) + _dot(tril, a_mid) + _dot(tril, a_lo)) * LOG2_E
    cs_t = cs.T
    for blk in range(tm // LANES):
        cst_ref[blk] = cs_t[:, blk * LANES:(blk + 1) * LANES]
    cs_hi, cs_lo = _split2(cs)
    dt_hi, dt_lo = _split2(dt)
    lane_t = lax.broadcasted_iota(jnp.int32, (tm, LANES), 1)
    packed = jnp.where(lane_t < SSM_HEADS, cs_hi,
                       jnp.where(lane_t < 2 * SSM_HEADS, cs_lo,
                                 jnp.where(lane_t < 3 * SSM_HEADS, dt_hi, dt_lo))).astype(BF16)

    lane_8 = lax.broadcasted_iota(jnp.int32, (SSM_GROUPS, LANES), 1)
    cs_rows = []
    for c in range(n_chunks):
        blk = c // 2
        pieces = [cst_ref[blk, pl.ds(hh, SSM_GROUPS, stride=HEADS_PER_GROUP), :]
                  for hh in range(HEADS_PER_GROUP)]
        rolled = [pltpu.roll(p, CHUNK, axis=1) for p in pieces]
        if c % 2 == 0:
            halves = [jnp.where(lane_8 < CHUNK, pieces[0], rolled[1]),
                      jnp.where(lane_8 < CHUNK, pieces[2], rolled[3])]
        else:
            halves = [jnp.where(lane_8 < CHUNK, rolled[0], pieces[1]),
                      jnp.where(lane_8 < CHUNK, rolled[2], pieces[3])]
        cs_rows.append(jnp.concatenate(halves, axis=1))

    a_proj = {}

    def project_a(j):
        a_proj[j] = project(wmain_ref, [(part * D_SHORT + j * A_BLOCK, A_BLOCK) for part in range(3)])

    def project_xbc(g):
        xbuf[g, SUBLANES:SUBLANES + tm, :] = project(
            wmain_ref, [(W_XBC_START + lo, width) for lo, width in xbc_pieces(g)])

    def project_z(g):
        zbuf[g] = project(wmain_ref, [(W_Z_START + g * GROUP_WIDTH, GROUP_WIDTH)])

    def conv_a(j):
        cols = slice(j * A_BLOCK, (j + 1) * A_BLOCK)
        p = a_proj.pop(j)
        u = p[:, A_BLOCK:2 * A_BLOCK] * p[:, 2 * A_BLOCK:]
        ubuf[j, SUBLANES:SUBLANES + tm, :] = u
        va = cwa_ref[SHORT_CONV - 1:SHORT_CONV, cols] * u
        for k in range(SHORT_CONV - 1):
            off = SUBLANES - (SHORT_CONV - 1) + k
            va = va + cwa_ref[k:k + 1, cols] * ubuf[j, off:off + tm, :]
        ubuf[j, 0:SUBLANES, :] = ubuf[j, tm:tm + SUBLANES, :]
        yap_ref[j] = (p[:, :A_BLOCK] * va).astype(BF16)

    lane_sq = lax.broadcasted_iota(jnp.int32, (LANES, LANES), 1)
    row_i = lax.broadcasted_iota(jnp.int32, (CHUNK, GROUP_WIDTH), 0)
    lane_j = lax.broadcasted_iota(jnp.int32, (CHUNK, GROUP_WIDTH), 1) % CHUNK
    causal = row_i >= lane_j
    bdmask = bdmask_ref[...]

    def group_stages(g):
        gcols = slice(g * GROUP_WIDTH, (g + 1) * GROUP_WIDTH)
        acc = (gather_xbc(cbb_ref, slice(None), g)
               + gather_xbc(cwb_ref, slice(SSM_CONV - 1, SSM_CONV), g) * xbuf[g, SUBLANES:SUBLANES + tm, :])
        for k in range(SSM_CONV - 1):
            off = SUBLANES - (SSM_CONV - 1) + k
            acc = acc + gather_xbc(cwb_ref, slice(k, k + 1), g) * xbuf[g, off:off + tm, :]
        xbuf[g, 0:SUBLANES, :] = xbuf[g, tm:tm + SUBLANES, :]
        v = acc * _sigmoid(acc)
        yield
        xs = v[:, :GROUP_WIDTH]
        cmg = v[:, GROUP_WIDTH + D_STATE:].astype(BF16)
        bm_t = v[:, GROUP_WIDTH:GROUP_WIDTH + D_STATE].T
        bt = []
        for blk in range(tm // LANES):
            w = bm_t[:, blk * LANES:(blk + 1) * LANES]
            r = pltpu.roll(w, CHUNK, axis=1)
            bt.append(jnp.where(lane_sq < CHUNK, w, r).astype(BF16))
            bt.append(jnp.where(lane_sq < CHUNK, r, w).astype(BF16))

        ex = _dot(packed, expand_ref[:, 2 * g * GROUP_WIDTH:2 * (g + 1) * GROUP_WIDTH])
        cscol = ex[:, :GROUP_WIDTH]
        xdt = xs * ex[:, GROUP_WIDTH:]
        yield

        state = st_ref[g]
        ys = []
        for c in range(n_chunks):
            rows = slice(c * CHUNK, (c + 1) * CHUNK)
            col = cscol[rows]
            xdt_c = xdt[rows]
            cm_c = cmg[rows]
            cb4 = _dot(cm_c, jnp.concatenate([bt[c], bt[c]], axis=1))
            decay = jnp.exp2(jnp.where(causal, col - cs_rows[c][g:g + 1, :], MASKED))
            m = (cb4 * decay).astype(BF16)
            xd = xdt_c.astype(BF16)
            bd = jnp.concatenate([xd] * HEADS_PER_GROUP, axis=0) * bdmask
            y_off = _dot(cm_c, state.astype(BF16)) * jnp.exp2(col)
            ys.append(_dot(m, bd) + y_off + xs[rows] * dexp_ref[:, gcols])
            last = col[CHUNK - 1:CHUNK, :]
            xp = (xdt_c * jnp.exp2(last - col)).astype(BF16)
            state = state * jnp.exp2(last) + _dot(bt[c], jnp.concatenate([xp, jnp.zeros_like(xp)], axis=0))
            yield
        st_ref[g] = state
        y = jnp.concatenate(ys, axis=0)

        z = zbuf[g]
        yz = y * (z * _sigmoid(z))
        yn = yz * lax.rsqrt(jnp.mean(yz * yz, axis=-1, keepdims=True) + NORM_EPS) * snorm_ref[:, gcols]
        yn_ref[g] = yn.astype(BF16)

    def ssd_out_partial(g_lo, g_hi):
        yn = jnp.concatenate([yn_ref[g] for g in range(g_lo, g_hi)], axis=1)
        return _dot(yn, sswo_ref[g_lo * GROUP_WIDTH:g_hi * GROUP_WIDTH, :])

    def project_gates(j):
        pgbuf[j] = project(wg_ref, [(part * D_MODEL + j * A_BLOCK, A_BLOCK) for part in range(2)])

    def project_ya():
        yabuf[...] = _dot(jnp.concatenate([yap_ref[j] for j in range(N_A_BLOCKS)], axis=1), swo_ref[...])

    def accumulate_yb(g_lo, g_hi):
        if g_lo == 0:
            ybbuf[...] = ssd_out_partial(g_lo, g_hi)
        else:
            ybbuf[...] += ssd_out_partial(g_lo, g_hi)

    fillers = []
    for j in range(N_A_BLOCKS):
        fillers += [functools.partial(project_a, j), functools.partial(conv_a, j)]
    fillers += [project_ya] + [functools.partial(project_gates, j) for j in range(N_A_BLOCKS)]
    for g in range(GROUPS_PROJECTED_AHEAD):
        project_xbc(g)
        project_z(g)
    dense = []
    for g in range(GROUPS_PROJECTED_AHEAD, SSM_GROUPS):
        dense.append((0, g, functools.partial(project_xbc, g)))
        dense.append((0, g, functools.partial(project_z, g)))
        if fillers:
            dense.append((0, None, fillers.pop(0)))
    dense += [(0, None, f) for f in fillers]
    for g in range(0, SSM_GROUPS, YB_GROUPS):
        dense.append((g + YB_GROUPS, None, functools.partial(accumulate_yb, g, g + YB_GROUPS)))

    live, started, finished, rounds = [], 0, 0, 0
    while started < SSM_GROUPS or live:
        if started < SSM_GROUPS and rounds % GROUP_STAGGER == 0:
            for entry in [e for e in dense if e[1] == started]:
                entry[2]()
                dense.remove(entry)
            live.append(group_stages(started))
            started += 1
        still_live = [stages for stages in live if next(stages, True) is None]
        finished += len(live) - len(still_live)
        live = still_live
        ready = [e for e in dense if e[0] <= finished]
        if ready:
            ready[0][2]()
            dense.remove(ready[0])
        rounds += 1
    for entry in dense:
        entry[2]()

    for j in range(N_A_BLOCKS):
        cols = slice(j * A_BLOCK, (j + 1) * A_BLOCK)
        pg = pgbuf[j]
        merged = _sigmoid(pg[:, :A_BLOCK]) * yabuf[:, cols] + _sigmoid(pg[:, A_BLOCK:]) * ybbuf[:, cols]
        mg_ref[j] = merged.astype(BF16)
    o_ref[...] = x + _dot(jnp.concatenate([mg_ref[j] for j in range(N_A_BLOCKS)], axis=1), wo_ref[...])


def _mixer_constants(tm):
    t = np.arange(tm)
    tril = ((t[:, None] >= t[None, :]) & (t[:, None] // CHUNK == t[None, :] // CHUNK))
    k = np.arange(LANES)[:, None]
    col = np.arange(2 * D_INNER)[None, :]
    group, within = col // (2 * GROUP_WIDTH), col % (2 * GROUP_WIDTH)
    head = group * HEADS_PER_GROUP + (within % GROUP_WIDTH) // SSM_HEAD_DIM
    half = within // GROUP_WIDTH
    expand = (k % SSM_HEADS == head) & ((k // (2 * SSM_HEADS)) == half)
    r = np.arange(GROUP_WIDTH)
    bdmask = (r[:, None] // SSM_HEAD_DIM) == (r[None, :] // SSM_HEAD_DIM)
    return (jnp.asarray(tril, BF16), jnp.asarray(expand, BF16), jnp.asarray(bdmask, BF16))


def _cast_transposed_body(src_ref, dst_ref):
    dst_ref[...] = src_ref[...].T.astype(BF16)


def _cast_transposed(w_t, width):
    k = w_t.shape[1]
    return pl.pallas_call(
        _cast_transposed_body,
        grid=(width // CAST_BLOCK,),
        in_specs=[pl.BlockSpec((CAST_BLOCK, k), lambda i: (i, 0))],
        out_specs=pl.BlockSpec((k, CAST_BLOCK), lambda i: (0, i)),
        out_shape=jax.ShapeDtypeStruct((k, width), BF16),
        compiler_params=pltpu.CompilerParams(dimension_semantics=("arbitrary",)),
        name="cast_transposed",
    )(w_t)


def _mixer(x2d, batch, seq, gain, w_in, conv_a, short_w_out, conv_b, conv_b_bias, dt_bias, a_log,
           d_skip, ssm_norm, ssm_w_out, w_out, *, tm=256):
    d = D_MODEL
    ns = seq // tm
    w_dt_start = W_XBC_START + D_XBC
    w_g_start = w_dt_start + SSM_HEADS
    w_t = w_in.T
    wmain = _cast_transposed(w_t, w_dt_start)
    wdt = jnp.tile(w_t[w_dt_start:w_g_start].T, (1, HEAD_COPIES)).astype(BF16)
    wg = _cast_transposed(w_t[w_g_start:], 2 * D_MODEL)
    dtb = jnp.tile(dt_bias, HEAD_COPIES).reshape(1, LANES)
    alog = jnp.tile(a_log, HEAD_COPIES).reshape(1, LANES)
    dexp = jnp.repeat(d_skip, SSM_HEAD_DIM).reshape(1, D_INNER)
    tril, expand, bdmask = _mixer_constants(tm)
    operands = [
        gain.reshape(1, d), wmain, wdt, wg,
        conv_a, short_w_out.astype(BF16), conv_b, conv_b_bias.reshape(1, D_XBC),
        dtb, alog, dexp,
        ssm_norm.reshape(1, D_INNER), ssm_w_out.astype(BF16), w_out.astype(BF16),
        tril, expand, bdmask,
    ]
    x_spec = pl.BlockSpec((tm, d), lambda b, s: (b * ns + s, 0))
    scratch = [
        pltpu.VMEM((tm, d), BF16),
        pltpu.VMEM((N_A_BLOCKS, SUBLANES + tm, A_BLOCK), F32),
        pltpu.VMEM((SSM_GROUPS, SUBLANES + tm, GROUP_XBC), F32),
        pltpu.VMEM((SSM_GROUPS, tm, GROUP_WIDTH), F32),
        pltpu.VMEM((SSM_GROUPS, D_STATE, GROUP_WIDTH), F32),
        pltpu.VMEM((tm // LANES, LANES, LANES), F32),
        pltpu.VMEM((N_A_BLOCKS, tm, A_BLOCK), BF16),
        pltpu.VMEM((SSM_GROUPS, tm, GROUP_WIDTH), BF16),
        pltpu.VMEM((N_A_BLOCKS, tm, A_BLOCK), BF16),
        pltpu.VMEM((N_A_BLOCKS, tm, 2 * A_BLOCK), F32),
        pltpu.VMEM((tm, d), F32),
        pltpu.VMEM((tm, d), F32),
    ]
    return pl.pallas_call(
        functools.partial(_mixer_body, tm=tm),
        grid=(batch, ns),
        in_specs=[x_spec] + [_resident(op.shape) for op in operands],
        out_specs=x_spec,
        out_shape=jax.ShapeDtypeStruct(x2d.shape, F32),
        scratch_shapes=scratch,
        compiler_params=pltpu.CompilerParams(
            dimension_semantics=("arbitrary", "arbitrary"), vmem_limit_bytes=VMEM_LIMIT_BYTES),
        name="mixer",
    )(x2d, *operands)


def kernel(x, ffn1_norm, ffn1_w_in, ffn1_w_out, mix_norm, w_in, short_conv_w, short_w_out, ssm_conv_w, ssm_conv_b, ssm_dt_bias, ssm_A_log, ssm_D, ssm_norm, ssm_w_out, w_out, ffn2_norm, ffn2_w_in, ffn2_w_out, final_norm):
    b, s, d = x.shape
    x2d = x.reshape(b * s, d)
    fin = final_norm.reshape(1, d)
    for l in range(ffn1_norm.shape[0]):
        x2d = _ffn(x2d, ffn1_norm[l].reshape(1, d), ffn1_w_in[l].astype(BF16),
                   ffn1_w_out[l].astype(BF16), fin, final_norm=False)
        x2d = _mixer(x2d, b, s, mix_norm[l], w_in[l], short_conv_w[l], short_w_out[l],
                     ssm_conv_w[l], ssm_conv_b[l], ssm_dt_bias[l], ssm_A_log[l], ssm_D[l],
                     ssm_norm[l], ssm_w_out[l], w_out[l])
        last = l == ffn1_norm.shape[0] - 1
        x2d = _ffn(x2d, ffn2_norm[l].reshape(1, d), ffn2_w_in[l].astype(BF16),
                   ffn2_w_out[l].astype(BF16), fin, final_norm=last)
    return x2d.reshape(b, s, d)
```

```python
import functools

import jax
import jax.numpy as jnp
import numpy as np
from jax import lax
from jax.experimental import pallas as pl
from jax.experimental.pallas import tpu as pltpu

D_MODEL = 1024
D_FF = 2816
D_SHORT = D_MODEL
SHORT_CONV = 3
D_INNER = 2048
SSM_HEADS = 32
SSM_HEAD_DIM = 64
SSM_GROUPS = 8
D_STATE = 128
SSM_CONV = 4
D_XBC = D_INNER + 2 * SSM_GROUPS * D_STATE
CHUNK = 64
NORM_EPS = 1e-5

HEADS_PER_GROUP = SSM_HEADS // SSM_GROUPS
GROUP_WIDTH = HEADS_PER_GROUP * SSM_HEAD_DIM
GROUP_XBC = GROUP_WIDTH + 2 * D_STATE
LANES = 128
SUBLANES = 8
HEAD_COPIES = LANES // SSM_HEADS
A_BLOCK = 256
N_A_BLOCKS = D_SHORT // A_BLOCK
GROUP_STAGGER = 3
DENSE_PIECES_PER_ROUND = (2, 1)
GROUPS_PROJECTED_AHEAD = 2
YB_GROUPS = 2
CAST_BLOCK = 512
W_Z_START = 3 * D_SHORT
W_XBC_START = W_Z_START + D_INNER

F32 = jnp.float32
BF16 = jnp.bfloat16

VMEM_LIMIT_BYTES = 58 * 1024 * 1024
MASKED = -1e30
LOG2_E = 1.4426950408889634


def _rms_norm(x, gain):
    ms = jnp.mean(x * x, axis=-1, keepdims=True)
    return x * lax.rsqrt(ms + NORM_EPS) * gain


def _sigmoid(x):
    return 1.0 / (1.0 + jnp.exp(-x))


def _dot(a, b):
    return jnp.dot(a, b, preferred_element_type=F32)


def _split2(x):
    hi = x.astype(BF16).astype(F32)
    lo = (x - hi).astype(BF16).astype(F32)
    return hi, lo


def _resident(shape):
    nd = len(shape)
    return pl.BlockSpec(shape, lambda *_: (0,) * nd, pipeline_mode=pl.Buffered(1))


def _ffn_body(x_ref, gain_ref, w_in_ref, w_out_ref, fin_ref, o_ref, *, ff_chunk, final_norm):
    x = x_ref[...]
    h = _rms_norm(x, gain_ref[...]).astype(BF16)
    acc = jnp.zeros(x.shape, F32)
    for j in range(D_FF // ff_chunk):
        lo = j * ff_chunk
        g = _dot(h, w_in_ref[:, lo:lo + ff_chunk])
        u = _dot(h, w_in_ref[:, D_FF + lo:D_FF + lo + ff_chunk])
        a = (g * _sigmoid(g) * u).astype(BF16)
        acc = acc + _dot(a, w_out_ref[lo:lo + ff_chunk, :])
    y = x + 0.5 * acc
    if final_norm:
        y = _rms_norm(y, fin_ref[...])
    o_ref[...] = y


def _ffn(x2d, gain, w_in, w_out, fin, *, final_norm, tm=1024, ff_chunk=256):
    t, d = x2d.shape
    body = functools.partial(_ffn_body, ff_chunk=ff_chunk, final_norm=final_norm)
    return pl.pallas_call(
        body,
        grid=(t // tm,),
        in_specs=[
            pl.BlockSpec((tm, d), lambda i: (i, 0)),
            _resident((1, d)),
            _resident(w_in.shape),
            _resident(w_out.shape),
            _resident((1, d)),
        ],
        out_specs=pl.BlockSpec((tm, d), lambda i: (i, 0)),
        out_shape=jax.ShapeDtypeStruct((t, d), F32),
        compiler_params=pltpu.CompilerParams(
            dimension_semantics=("arbitrary",), vmem_limit_bytes=VMEM_LIMIT_BYTES),
        name="ffn_final" if final_norm else "ffn",
    )(x2d, gain, w_in, w_out, fin)


def _mixer_body(x_ref, gain_ref, wmain_ref, wdt_ref, wg_ref,
                cwa_ref, swo_ref, cwb_ref, cbb_ref, dtb_ref, alog_ref, dexp_ref,
                snorm_ref, sswo_ref, wo_ref, tril_ref, expand_ref, bdmask_ref,
                o_ref,
                hb_ref, ubuf, xbuf, zbuf, st_ref, tr_ref, yap_ref, yn_ref, mg_ref, pgbuf, yabuf, ybbuf,
                *, tm):
    n_chunks = tm // CHUNK

    @pl.when(pl.program_id(1) == 0)
    def _():
        ubuf[:, 0:SUBLANES, :] = jnp.zeros((N_A_BLOCKS, SUBLANES, A_BLOCK), F32)
        xbuf[:, 0:SUBLANES, :] = jnp.zeros((SSM_GROUPS, SUBLANES, GROUP_XBC), F32)
        st_ref[...] = jnp.zeros(st_ref.shape, F32)

    x = x_ref[...]
    hb_ref[...] = _rms_norm(x, gain_ref[...]).astype(BF16)

    def project(w_ref, pieces):
        w = [w_ref[:, lo:lo + width] for lo, width in pieces]
        return _dot(hb_ref[...], w[0] if len(w) == 1 else jnp.concatenate(w, axis=1))

    def xbc_pieces(g):
        return [(g * GROUP_WIDTH, GROUP_WIDTH),
                (D_INNER + g * D_STATE, D_STATE),
                (D_INNER + (SSM_GROUPS + g) * D_STATE, D_STATE)]

    def gather_xbc(ref, rows, g):
        return jnp.concatenate([ref[rows, lo:lo + width] for lo, width in xbc_pieces(g)], axis=1)

    pre = _dot(hb_ref[...], wdt_ref[...]) + dtb_ref[...]
    dt = jnp.maximum(pre, 0.0) + jnp.log1p(jnp.exp(-jnp.abs(pre)))
    a = dt * (-jnp.exp(alog_ref[...]))
    a_hi = a.astype(BF16)
    a_r1 = a - a_hi.astype(F32)
    a_mid = a_r1.astype(BF16)
    a_lo = (a_r1 - a_mid.astype(F32)).astype(BF16)
    tril = tril_ref[...]
    cs = (_dot(tril, a_hi) + _dot(tril, a_mid) + _dot(tril, a_lo)) * LOG2_E
    cs_last = jnp.concatenate(
        [jnp.broadcast_to(cs[(c + 1) * CHUNK - 1:(c + 1) * CHUNK, :], (CHUNK, LANES)) for c in range(n_chunks)], axis=0)
    to_end = dt * jnp.exp2(cs_last - cs)
    cs_hi, cs_lo = _split2(cs)
    lane_t = lax.broadcasted_iota(jnp.int32, (tm, LANES), 1)
    packed = jnp.where(lane_t < SSM_HEADS, cs_hi, jnp.where(lane_t < 2 * SSM_HEADS, cs_lo, 0.0)).astype(BF16)

    lane_8 = lax.broadcasted_iota(jnp.int32, (SSM_GROUPS, LANES), 1)

    def along_lanes(v, slot):
        v_t = v.T
        for blk in range(tm // LANES):
            tr_ref[slot, blk] = v_t[:, blk * LANES:(blk + 1) * LANES]
        out = []
        for c in range(n_chunks):
            blk = c // 2
            pieces = [tr_ref[slot, blk, pl.ds(hh, SSM_GROUPS, stride=HEADS_PER_GROUP), :]
                      for hh in range(HEADS_PER_GROUP)]
            rolled = [pltpu.roll(p, CHUNK, axis=1) for p in pieces]
            if c % 2 == 0:
                halves = [jnp.where(lane_8 < CHUNK, pieces[0], rolled[1]),
                          jnp.where(lane_8 < CHUNK, pieces[2], rolled[3])]
            else:
                halves = [jnp.where(lane_8 < CHUNK, rolled[0], pieces[1]),
                          jnp.where(lane_8 < CHUNK, rolled[2], pieces[3])]
            out.append(jnp.concatenate(halves, axis=1))
        return out

    cs_rows = along_lanes(cs, 0)
    dt_rows = along_lanes(dt, 1)
    to_end_rows = along_lanes(to_end, 2)

    a_proj = {}

    def project_a(j):
        a_proj[j] = project(wmain_ref, [(part * D_SHORT + j * A_BLOCK, A_BLOCK) for part in range(3)])

    def project_xbc(g):
        xbuf[g, SUBLANES:SUBLANES + tm, :] = project(
            wmain_ref, [(W_XBC_START + lo, width) for lo, width in xbc_pieces(g)])

    def project_z(g):
        zbuf[g] = project(wmain_ref, [(W_Z_START + g * GROUP_WIDTH, GROUP_WIDTH)])

    def conv_a(j):
        cols = slice(j * A_BLOCK, (j + 1) * A_BLOCK)
        p = a_proj.pop(j)
        u = p[:, A_BLOCK:2 * A_BLOCK] * p[:, 2 * A_BLOCK:]
        ubuf[j, SUBLANES:SUBLANES + tm, :] = u
        va = cwa_ref[SHORT_CONV - 1:SHORT_CONV, cols] * u
        for k in range(SHORT_CONV - 1):
            off = SUBLANES - (SHORT_CONV - 1) + k
            va = va + cwa_ref[k:k + 1, cols] * ubuf[j, off:off + tm, :]
        ubuf[j, 0:SUBLANES, :] = ubuf[j, tm:tm + SUBLANES, :]
        yap_ref[j] = (p[:, :A_BLOCK] * va).astype(BF16)

    lane_sq = lax.broadcasted_iota(jnp.int32, (LANES, LANES), 1)
    row_i = lax.broadcasted_iota(jnp.int32, (CHUNK, GROUP_WIDTH), 0)
    lane_j = lax.broadcasted_iota(jnp.int32, (CHUNK, GROUP_WIDTH), 1) % CHUNK
    causal = row_i >= lane_j
    bdmask = bdmask_ref[...]

    def group_stages(g):
        gcols = slice(g * GROUP_WIDTH, (g + 1) * GROUP_WIDTH)
        acc = (gather_xbc(cbb_ref, slice(None), g)
               + gather_xbc(cwb_ref, slice(SSM_CONV - 1, SSM_CONV), g) * xbuf[g, SUBLANES:SUBLANES + tm, :])
        for k in range(SSM_CONV - 1):
            off = SUBLANES - (SSM_CONV - 1) + k
            acc = acc + gather_xbc(cwb_ref, slice(k, k + 1), g) * xbuf[g, off:off + tm, :]
        xbuf[g, 0:SUBLANES, :] = xbuf[g, tm:tm + SUBLANES, :]
        v = acc * _sigmoid(acc)
        yield
        xs = v[:, :GROUP_WIDTH]
        xs_bf = xs.astype(BF16)
        cmg = v[:, GROUP_WIDTH + D_STATE:].astype(BF16)
        bm_t = v[:, GROUP_WIDTH:GROUP_WIDTH + D_STATE].T
        bt = []
        for blk in range(tm // LANES):
            w = bm_t[:, blk * LANES:(blk + 1) * LANES]
            r = pltpu.roll(w, CHUNK, axis=1)
            bt.append(jnp.where(lane_sq < CHUNK, w, r))
            bt.append(jnp.where(lane_sq < CHUNK, r, w))
        cscol = _dot(packed, expand_ref[:, gcols])
        yield

        state = st_ref[g]
        ys = []
        for c in range(n_chunks):
            rows = slice(c * CHUNK, (c + 1) * CHUNK)
            col = cscol[rows]
            cm_c = cmg[rows]
            bt4 = jnp.concatenate([bt[c], bt[c]], axis=1)
            cb4 = _dot(cm_c, bt4.astype(BF16))
            decay = jnp.exp2(jnp.where(causal, col - cs_rows[c][g:g + 1, :], MASKED))
            m = (cb4 * decay * dt_rows[c][g:g + 1, :]).astype(BF16)
            b_end = (bt4 * to_end_rows[c][g:g + 1, :]).astype(BF16)
            bd = jnp.concatenate([xs_bf[rows]] * HEADS_PER_GROUP, axis=0) * bdmask
            both = _dot(jnp.concatenate([m, b_end], axis=0), bd)
            y_off = _dot(cm_c, state.astype(BF16)) * jnp.exp2(col)
            ys.append(both[:CHUNK] + y_off + xs[rows] * dexp_ref[:, gcols])
            state = state * jnp.exp2(col[CHUNK - 1:CHUNK, :]) + both[CHUNK:]
            yield
        st_ref[g] = state
        y = jnp.concatenate(ys, axis=0)

        z = zbuf[g]
        yz = y * (z * _sigmoid(z))
        yn = yz * lax.rsqrt(jnp.mean(yz * yz, axis=-1, keepdims=True) + NORM_EPS) * snorm_ref[:, gcols]
        yn_ref[g] = yn.astype(BF16)

    def ssd_out_partial(g_lo, g_hi):
        yn = jnp.concatenate([yn_ref[g] for g in range(g_lo, g_hi)], axis=1)
        return _dot(yn, sswo_ref[g_lo * GROUP_WIDTH:g_hi * GROUP_WIDTH, :])

    def project_gates(j):
        pgbuf[j] = project(wg_ref, [(part * D_MODEL + j * A_BLOCK, A_BLOCK) for part in range(2)])

    def project_ya():
        yabuf[...] = _dot(jnp.concatenate([yap_ref[j] for j in range(N_A_BLOCKS)], axis=1), swo_ref[...])

    def accumulate_yb(g_lo, g_hi):
        if g_lo == 0:
            ybbuf[...] = ssd_out_partial(g_lo, g_hi)
        else:
            ybbuf[...] += ssd_out_partial(g_lo, g_hi)

    fillers = []
    for j in range(N_A_BLOCKS):
        fillers += [functools.partial(project_a, j), functools.partial(conv_a, j)]
    fillers += [project_ya] + [functools.partial(project_gates, j) for j in range(N_A_BLOCKS)]
    for g in range(GROUPS_PROJECTED_AHEAD):
        project_xbc(g)
        project_z(g)
    dense = []
    for g in range(GROUPS_PROJECTED_AHEAD, SSM_GROUPS):
        dense.append((0, g, functools.partial(project_xbc, g)))
        dense.append((0, g, functools.partial(project_z, g)))
        if fillers:
            dense.append((0, None, fillers.pop(0)))
    dense += [(0, None, f) for f in fillers]
    for g in range(0, SSM_GROUPS, YB_GROUPS):
        dense.append((g + YB_GROUPS, None, functools.partial(accumulate_yb, g, g + YB_GROUPS)))

    live, started, finished, rounds = [], 0, 0, 0
    while started < SSM_GROUPS or live:
        if started < SSM_GROUPS and rounds % GROUP_STAGGER == 0:
            for entry in [e for e in dense if e[1] == started]:
                entry[2]()
                dense.remove(entry)
            live.append(group_stages(started))
            started += 1
        still_live = [stages for stages in live if next(stages, True) is None]
        finished += len(live) - len(still_live)
        live = still_live
        for entry in [e for e in dense if e[0] <= finished][:DENSE_PIECES_PER_ROUND[rounds % 2]]:
            entry[2]()
            dense.remove(entry)
        rounds += 1
    for entry in dense:
        entry[2]()

    for j in range(N_A_BLOCKS):
        cols = slice(j * A_BLOCK, (j + 1) * A_BLOCK)
        pg = pgbuf[j]
        merged = _sigmoid(pg[:, :A_BLOCK]) * yabuf[:, cols] + _sigmoid(pg[:, A_BLOCK:]) * ybbuf[:, cols]
        mg_ref[j] = merged.astype(BF16)
    o_ref[...] = x + _dot(jnp.concatenate([mg_ref[j] for j in range(N_A_BLOCKS)], axis=1), wo_ref[...])


def _mixer_constants(tm):
    t = np.arange(tm)
    tril = ((t[:, None] >= t[None, :]) & (t[:, None] // CHUNK == t[None, :] // CHUNK))
    k = np.arange(LANES)[:, None]
    head = np.arange(D_INNER)[None, :] // SSM_HEAD_DIM
    expand = (k % SSM_HEADS == head) & (k < 2 * SSM_HEADS)
    r = np.arange(GROUP_WIDTH)
    bdmask = (r[:, None] // SSM_HEAD_DIM) == (r[None, :] // SSM_HEAD_DIM)
    return (jnp.asarray(tril, BF16), jnp.asarray(expand, BF16), jnp.asarray(bdmask, BF16))


def _cast_transposed_body(src_ref, dst_ref):
    dst_ref[...] = src_ref[...].T.astype(BF16)


def _cast_transposed(w_t, width):
    k = w_t.shape[1]
    return pl.pallas_call(
        _cast_transposed_body,
        grid=(width // CAST_BLOCK,),
        in_specs=[pl.BlockSpec((CAST_BLOCK, k), lambda i: (i, 0))],
        out_specs=pl.BlockSpec((k, CAST_BLOCK), lambda i: (0, i)),
        out_shape=jax.ShapeDtypeStruct((k, width), BF16),
        compiler_params=pltpu.CompilerParams(dimension_semantics=("arbitrary",)),
        name="cast_transposed",
    )(w_t)


def _mixer(x2d, batch, seq, gain, w_in, conv_a, short_w_out, conv_b, conv_b_bias, dt_bias, a_log,
           d_skip, ssm_norm, ssm_w_out, w_out, *, tm=256):
    d = D_MODEL
    ns = seq // tm
    w_dt_start = W_XBC_START + D_XBC
    w_g_start = w_dt_start + SSM_HEADS
    w_t = w_in.T
    wmain = _cast_transposed(w_t, w_dt_start)
    wdt = jnp.tile(w_t[w_dt_start:w_g_start].T, (1, HEAD_COPIES)).astype(BF16)
    wg = _cast_transposed(w_t[w_g_start:], 2 * D_MODEL)
    dtb = jnp.tile(dt_bias, HEAD_COPIES).reshape(1, LANES)
    alog = jnp.tile(a_log, HEAD_COPIES).reshape(1, LANES)
    dexp = jnp.repeat(d_skip, SSM_HEAD_DIM).reshape(1, D_INNER)
    tril, expand, bdmask = _mixer_constants(tm)
    operands = [
        gain.reshape(1, d), wmain, wdt, wg,
        conv_a, short_w_out.astype(BF16), conv_b, conv_b_bias.reshape(1, D_XBC),
        dtb, alog, dexp,
        ssm_norm.reshape(1, D_INNER), ssm_w_out.astype(BF16), w_out.astype(BF16),
        tril, expand, bdmask,
    ]
    x_spec = pl.BlockSpec((tm, d), lambda b, s: (b * ns + s, 0))
    scratch = [
        pltpu.VMEM((tm, d), BF16),
        pltpu.VMEM((N_A_BLOCKS, SUBLANES + tm, A_BLOCK), F32),
        pltpu.VMEM((SSM_GROUPS, SUBLANES + tm, GROUP_XBC), F32),
        pltpu.VMEM((SSM_GROUPS, tm, GROUP_WIDTH), F32),
        pltpu.VMEM((SSM_GROUPS, D_STATE, GROUP_WIDTH), F32),
        pltpu.VMEM((3, tm // LANES, LANES, LANES), F32),
        pltpu.VMEM((N_A_BLOCKS, tm, A_BLOCK), BF16),
        pltpu.VMEM((SSM_GROUPS, tm, GROUP_WIDTH), BF16),
        pltpu.VMEM((N_A_BLOCKS, tm, A_BLOCK), BF16),
        pltpu.VMEM((N_A_BLOCKS, tm, 2 * A_BLOCK), F32),
        pltpu.VMEM((tm, d), F32),
        pltpu.VMEM((tm, d), F32),
    ]
    return pl.pallas_call(
        functools.partial(_mixer_body, tm=tm),
        grid=(batch, ns),
        in_specs=[x_spec] + [_resident(op.shape) for op in operands],
        out_specs=x_spec,
        out_shape=jax.ShapeDtypeStruct(x2d.shape, F32),
        scratch_shapes=scratch,
        compiler_params=pltpu.CompilerParams(
            dimension_semantics=("arbitrary", "arbitrary"), vmem_limit_bytes=VMEM_LIMIT_BYTES),
        name="mixer",
    )(x2d, *operands)


def kernel(x, ffn1_norm, ffn1_w_in, ffn1_w_out, mix_norm, w_in, short_conv_w, short_w_out, ssm_conv_w, ssm_conv_b, ssm_dt_bias, ssm_A_log, ssm_D, ssm_norm, ssm_w_out, w_out, ffn2_norm, ffn2_w_in, ffn2_w_out, final_norm):
    b, s, d = x.shape
    x2d = x.reshape(b * s, d)
    fin = final_norm.reshape(1, d)
    for l in range(ffn1_norm.shape[0]):
        x2d = _ffn(x2d, ffn1_norm[l].reshape(1, d), ffn1_w_in[l].astype(BF16),
                   ffn1_w_out[l].astype(BF16), fin, final_norm=False)
        x2d = _mixer(x2d, b, s, mix_norm[l], w_in[l], short_conv_w[l], short_w_out[l],
                     ssm_conv_w[l], ssm_conv_b[l], ssm_dt_bias[l], ssm_A_log[l], ssm_D[l],
                     ssm_norm[l], ssm_w_out[l], w_out[l])
        last = l == ffn1_norm.shape[0] - 1
        x2d = _ffn(x2d, ffn2_norm[l].reshape(1, d), ffn2_w_in[l].astype(BF16),
                   ffn2_w_out[l].astype(BF16), fin, final_norm=last)
    return x2d.reshape(b, s, d)
```

```python
import functools
from typing import NamedTuple

import jax
import jax.numpy as jnp
import numpy as np
from jax import lax
from jax.experimental import pallas as pl
from jax.experimental.pallas import tpu as pltpu

D_MODEL = 1024
D_FF = 2816
D_SHORT = D_MODEL
SHORT_CONV = 3
D_INNER = 2048
SSM_HEADS = 32
SSM_HEAD_DIM = 64
SSM_GROUPS = 8
D_STATE = 128
SSM_CONV = 4
D_XBC = D_INNER + 2 * SSM_GROUPS * D_STATE
CHUNK = 64
NORM_EPS = 1e-5

HEADS_PER_GROUP = SSM_HEADS // SSM_GROUPS
GROUP_WIDTH = HEADS_PER_GROUP * SSM_HEAD_DIM
GROUP_XBC = GROUP_WIDTH + 2 * D_STATE
LANES = 128
SUBLANES = 8
HEAD_COPIES = LANES // SSM_HEADS
A_BLOCK = 256
N_A_BLOCKS = D_SHORT // A_BLOCK
GROUP_STAGGER = 3
DENSE_PIECES_PER_ROUND = (2, 1)
GROUPS_PROJECTED_AHEAD = 2
YB_GROUPS = 2
CAST_BLOCK = 512
W_Z_START = 3 * D_SHORT
W_XBC_START = W_Z_START + D_INNER

F32 = jnp.float32
BF16 = jnp.bfloat16

VMEM_LIMIT_BYTES = 58 * 1024 * 1024
MASKED = -1e30
LOG2_E = 1.4426950408889634


def _rms_norm(x, gain):
    ms = jnp.mean(x * x, axis=-1, keepdims=True)
    return x * lax.rsqrt(ms + NORM_EPS) * gain


def _sigmoid(x):
    return 1.0 / (1.0 + jnp.exp(-x))


def _dot(a, b):
    return jnp.dot(a, b, preferred_element_type=F32)


def _split2(x):
    hi = x.astype(BF16).astype(F32)
    lo = (x - hi).astype(BF16).astype(F32)
    return hi, lo


def _resident(shape):
    nd = len(shape)
    return pl.BlockSpec(shape, lambda *_: (0,) * nd, pipeline_mode=pl.Buffered(1))


class _CastJob(NamedTuple):
    src: jax.Array
    block: tuple
    axis: int
    transpose: bool
    start: int = 0
    count: int = 0

    @property
    def extent(self):
        return self.count or self.src.shape[self.axis] - self.start

    @property
    def n_blocks(self):
        return self.extent // self.block[self.axis]

    def specs(self):
        n, axis, block = self.n_blocks, self.axis, self.block
        step = block[axis]

        def src_index(i):
            k = jnp.minimum(i, n - 1)
            if self.start:
                assert self.start % SUBLANES == 0 and step % SUBLANES == 0
                return (pl.multiple_of(self.start + k * step, SUBLANES), 0)
            return (k, 0) if axis == 0 else (0, k)

        def dst_index(i):
            k = jnp.minimum(i, n - 1)
            return (k, 0) if (axis == 0) != self.transpose else (0, k)

        src_block = (pl.Element(block[0]), pl.Element(block[1])) if self.start else block
        dst_block = block[::-1] if self.transpose else block
        rows, cols = self.src.shape
        rows, cols = (self.extent, cols) if axis == 0 else (rows, self.extent)
        dst_shape = (cols, rows) if self.transpose else (rows, cols)
        return (pl.BlockSpec(src_block, src_index), pl.BlockSpec(dst_block, dst_index),
                jax.ShapeDtypeStruct(dst_shape, BF16))


def _ffn_body(x_ref, gain_ref, w_in_ref, w_out_ref, fin_ref, *rest, ff_chunk, final_norm, casts):
    cast_src, o_ref, cast_dst = rest[:len(casts)], rest[len(casts)], rest[len(casts) + 1:]
    pending = list(zip(casts, cast_src, cast_dst))

    x = x_ref[...]
    h = _rms_norm(x, gain_ref[...]).astype(BF16)
    acc = jnp.zeros(x.shape, F32)
    n_chunks = D_FF // ff_chunk
    for j in range(n_chunks):
        lo = j * ff_chunk
        g = _dot(h, w_in_ref[:, lo:lo + ff_chunk])
        u = _dot(h, w_in_ref[:, D_FF + lo:D_FF + lo + ff_chunk])
        a = (g * _sigmoid(g) * u).astype(BF16)
        acc = acc + _dot(a, w_out_ref[lo:lo + ff_chunk, :])
        for transpose, src_ref, dst_ref in (pending[j:j + 1] if j < n_chunks - 1 else pending[j:]):
            v = src_ref[...]
            dst_ref[...] = (v.T if transpose else v).astype(BF16)
    y = x + 0.5 * acc
    if final_norm:
        y = _rms_norm(y, fin_ref[...])
    o_ref[...] = y


def _ffn(x2d, gain, w_in, w_out, fin, *, final_norm, tm, ff_chunk=256, jobs=()):
    t, d = x2d.shape
    n_steps = t // tm
    assert all(job.n_blocks <= n_steps for job in jobs)
    job_specs = [job.specs() for job in jobs]
    body = functools.partial(_ffn_body, ff_chunk=ff_chunk, final_norm=final_norm,
                             casts=tuple(job.transpose for job in jobs))
    x_spec = pl.BlockSpec((tm, d), lambda i: (i, 0))
    outs = pl.pallas_call(
        body,
        grid=(n_steps,),
        in_specs=[x_spec, _resident((1, d)), _resident(w_in.shape), _resident(w_out.shape), _resident((1, d))]
        + [s[0] for s in job_specs],
        out_specs=[x_spec] + [s[1] for s in job_specs],
        out_shape=[jax.ShapeDtypeStruct((t, d), F32)] + [s[2] for s in job_specs],
        compiler_params=pltpu.CompilerParams(
            dimension_semantics=("arbitrary",), vmem_limit_bytes=VMEM_LIMIT_BYTES),
        name="ffn_final" if final_norm else "ffn",
    )(x2d, gain, w_in, w_out, fin, *[job.src for job in jobs])
    return outs[0], outs[1:]


def _mixer_body(x_ref, gain_ref, wmain_ref, wdt_ref, wg_ref,
                cwa_ref, swo_ref, cwb_ref, cbb_ref, dtb_ref, alog_ref, dexp_ref,
                snorm_ref, sswo_ref, wo_ref, tril_ref, expand_ref, bdmask_ref,
                o_ref,
                hb_ref, ubuf, xbuf, zbuf, st_ref, tr_ref, yap_ref, yn_ref, mg_ref, pgbuf, yabuf, ybbuf,
                *, tm):
    n_chunks = tm // CHUNK

    @pl.when(pl.program_id(1) == 0)
    def _():
        ubuf[:, 0:SUBLANES, :] = jnp.zeros((N_A_BLOCKS, SUBLANES, A_BLOCK), F32)
        xbuf[:, 0:SUBLANES, :] = jnp.zeros((SSM_GROUPS, SUBLANES, GROUP_XBC), F32)
        st_ref[...] = jnp.zeros(st_ref.shape, F32)

    x = x_ref[...]
    hb_ref[...] = _rms_norm(x, gain_ref[...]).astype(BF16)

    def project(w_ref, pieces):
        w = [w_ref[:, lo:lo + width] for lo, width in pieces]
        return _dot(hb_ref[...], w[0] if len(w) == 1 else jnp.concatenate(w, axis=1))

    def xbc_pieces(g):
        return [(g * GROUP_WIDTH, GROUP_WIDTH),
                (D_INNER + g * D_STATE, D_STATE),
                (D_INNER + (SSM_GROUPS + g) * D_STATE, D_STATE)]

    def gather_xbc(ref, rows, g):
        return jnp.concatenate([ref[rows, lo:lo + width] for lo, width in xbc_pieces(g)], axis=1)

    pre = _dot(hb_ref[...], wdt_ref[...]) + dtb_ref[...]
    dt = jnp.maximum(pre, 0.0) + jnp.log1p(jnp.exp(-jnp.abs(pre)))
    a = dt * (-jnp.exp(alog_ref[...]))
    a_hi = a.astype(BF16)
    a_r1 = a - a_hi.astype(F32)
    a_mid = a_r1.astype(BF16)
    a_lo = (a_r1 - a_mid.astype(F32)).astype(BF16)
    tril = tril_ref[...]
    cs = (_dot(tril, a_hi) + _dot(tril, a_mid) + _dot(tril, a_lo)) * LOG2_E
    cs_last = jnp.concatenate(
        [jnp.broadcast_to(cs[(c + 1) * CHUNK - 1:(c + 1) * CHUNK, :], (CHUNK, LANES)) for c in range(n_chunks)], axis=0)
    to_end = dt * jnp.exp2(cs_last - cs)
    cs_hi, cs_lo = _split2(cs)
    lane_t = lax.broadcasted_iota(jnp.int32, (tm, LANES), 1)
    packed = jnp.where(lane_t < SSM_HEADS, cs_hi, jnp.where(lane_t < 2 * SSM_HEADS, cs_lo, 0.0)).astype(BF16)

    lane_8 = lax.broadcasted_iota(jnp.int32, (SSM_GROUPS, LANES), 1)

    def along_lanes(v, slot):
        v_t = v.T
        for blk in range(tm // LANES):
            tr_ref[slot, blk] = v_t[:, blk * LANES:(blk + 1) * LANES]
        out = []
        for c in range(n_chunks):
            blk = c // 2
            pieces = [tr_ref[slot, blk, pl.ds(hh, SSM_GROUPS, stride=HEADS_PER_GROUP), :]
                      for hh in range(HEADS_PER_GROUP)]
            rolled = [pltpu.roll(p, CHUNK, axis=1) for p in pieces]
            if c % 2 == 0:
                halves = [jnp.where(lane_8 < CHUNK, pieces[0], rolled[1]),
                          jnp.where(lane_8 < CHUNK, pieces[2], rolled[3])]
            else:
                halves = [jnp.where(lane_8 < CHUNK, rolled[0], pieces[1]),
                          jnp.where(lane_8 < CHUNK, rolled[2], pieces[3])]
            out.append(jnp.concatenate(halves, axis=1))
        return out

    cs_rows = along_lanes(cs, 0)
    dt_rows = along_lanes(dt, 1)
    to_end_rows = along_lanes(to_end, 2)

    a_proj = {}

    def project_a(j):
        a_proj[j] = project(wmain_ref, [(part * D_SHORT + j * A_BLOCK, A_BLOCK) for part in range(3)])

    def project_xbc(g):
        xbuf[g, SUBLANES:SUBLANES + tm, :] = project(
            wmain_ref, [(W_XBC_START + lo, width) for lo, width in xbc_pieces(g)])

    def project_z(g):
        zbuf[g] = project(wmain_ref, [(W_Z_START + g * GROUP_WIDTH, GROUP_WIDTH)])

    def conv_a(j):
        cols = slice(j * A_BLOCK, (j + 1) * A_BLOCK)
        p = a_proj.pop(j)
        u = p[:, A_BLOCK:2 * A_BLOCK] * p[:, 2 * A_BLOCK:]
        ubuf[j, SUBLANES:SUBLANES + tm, :] = u
        va = cwa_ref[SHORT_CONV - 1:SHORT_CONV, cols] * u
        for k in range(SHORT_CONV - 1):
            off = SUBLANES - (SHORT_CONV - 1) + k
            va = va + cwa_ref[k:k + 1, cols] * ubuf[j, off:off + tm, :]
        ubuf[j, 0:SUBLANES, :] = ubuf[j, tm:tm + SUBLANES, :]
        yap_ref[j] = (p[:, :A_BLOCK] * va).astype(BF16)

    lane_sq = lax.broadcasted_iota(jnp.int32, (LANES, LANES), 1)
    row_i = lax.broadcasted_iota(jnp.int32, (CHUNK, GROUP_WIDTH), 0)
    lane_j = lax.broadcasted_iota(jnp.int32, (CHUNK, GROUP_WIDTH), 1) % CHUNK
    causal = row_i >= lane_j
    bdmask = bdmask_ref[...]

    def group_stages(g):
        gcols = slice(g * GROUP_WIDTH, (g + 1) * GROUP_WIDTH)
        acc = (gather_xbc(cbb_ref, slice(None), g)
               + gather_xbc(cwb_ref, slice(SSM_CONV - 1, SSM_CONV), g) * xbuf[g, SUBLANES:SUBLANES + tm, :])
        for k in range(SSM_CONV - 1):
            off = SUBLANES - (SSM_CONV - 1) + k
            acc = acc + gather_xbc(cwb_ref, slice(k, k + 1), g) * xbuf[g, off:off + tm, :]
        xbuf[g, 0:SUBLANES, :] = xbuf[g, tm:tm + SUBLANES, :]
        v = acc * _sigmoid(acc)
        yield
        xs = v[:, :GROUP_WIDTH]
        xs_bf = xs.astype(BF16)
        cmg = v[:, GROUP_WIDTH + D_STATE:].astype(BF16)
        bm_t = v[:, GROUP_WIDTH:GROUP_WIDTH + D_STATE].T
        bt = []
        for blk in range(tm // LANES):
            w = bm_t[:, blk * LANES:(blk + 1) * LANES]
            r = pltpu.roll(w, CHUNK, axis=1)
            bt.append(jnp.where(lane_sq < CHUNK, w, r))
            bt.append(jnp.where(lane_sq < CHUNK, r, w))
        cscol = _dot(packed, expand_ref[:, gcols])
        yield

        state = st_ref[g]
        ys = []
        for c in range(n_chunks):
            rows = slice(c * CHUNK, (c + 1) * CHUNK)
            col = cscol[rows]
            cm_c = cmg[rows]
            bt4 = jnp.concatenate([bt[c], bt[c]], axis=1)
            cb4 = _dot(cm_c, bt4.astype(BF16))
            decay = jnp.exp2(jnp.where(causal, col - cs_rows[c][g:g + 1, :], MASKED))
            m = (cb4 * decay * dt_rows[c][g:g + 1, :]).astype(BF16)
            b_end = (bt4 * to_end_rows[c][g:g + 1, :]).astype(BF16)
            bd = jnp.concatenate([xs_bf[rows]] * HEADS_PER_GROUP, axis=0) * bdmask
            both = _dot(jnp.concatenate([m, b_end], axis=0), bd)
            y_off = _dot(cm_c, state.astype(BF16)) * jnp.exp2(col)
            ys.append(both[:CHUNK] + y_off + xs[rows] * dexp_ref[:, gcols])
            state = state * jnp.exp2(col[CHUNK - 1:CHUNK, :]) + both[CHUNK:]
            yield
        st_ref[g] = state
        y = jnp.concatenate(ys, axis=0)

        z = zbuf[g]
        yz = y * (z * _sigmoid(z))
        yn = yz * lax.rsqrt(jnp.mean(yz * yz, axis=-1, keepdims=True) + NORM_EPS) * snorm_ref[:, gcols]
        yn_ref[g] = yn.astype(BF16)

    def ssd_out_partial(g_lo, g_hi):
        yn = jnp.concatenate([yn_ref[g] for g in range(g_lo, g_hi)], axis=1)
        return _dot(yn, sswo_ref[g_lo * GROUP_WIDTH:g_hi * GROUP_WIDTH, :])

    def project_gates(j):
        pgbuf[j] = project(wg_ref, [(part * D_MODEL + j * A_BLOCK, A_BLOCK) for part in range(2)])

    def project_ya():
        yabuf[...] = _dot(jnp.concatenate([yap_ref[j] for j in range(N_A_BLOCKS)], axis=1), swo_ref[...])

    def accumulate_yb(g_lo, g_hi):
        if g_lo == 0:
            ybbuf[...] = ssd_out_partial(g_lo, g_hi)
        else:
            ybbuf[...] += ssd_out_partial(g_lo, g_hi)

    fillers = []
    for j in range(N_A_BLOCKS):
        fillers += [functools.partial(project_a, j), functools.partial(conv_a, j)]
    fillers += [project_ya] + [functools.partial(project_gates, j) for j in range(N_A_BLOCKS)]
    for g in range(GROUPS_PROJECTED_AHEAD):
        project_xbc(g)
        project_z(g)
    dense = []
    for g in range(GROUPS_PROJECTED_AHEAD, SSM_GROUPS):
        dense.append((0, g, functools.partial(project_xbc, g)))
        dense.append((0, g, functools.partial(project_z, g)))
        if fillers:
            dense.append((0, None, fillers.pop(0)))
    dense += [(0, None, f) for f in fillers]
    for g in range(0, SSM_GROUPS, YB_GROUPS):
        dense.append((g + YB_GROUPS, None, functools.partial(accumulate_yb, g, g + YB_GROUPS)))

    live, started, finished, rounds = [], 0, 0, 0
    while started < SSM_GROUPS or live:
        if started < SSM_GROUPS and rounds % GROUP_STAGGER == 0:
            for entry in [e for e in dense if e[1] == started]:
                entry[2]()
                dense.remove(entry)
            live.append(group_stages(started))
            started += 1
        still_live = [stages for stages in live if next(stages, True) is None]
        finished += len(live) - len(still_live)
        live = still_live
        for entry in [e for e in dense if e[0] <= finished][:DENSE_PIECES_PER_ROUND[rounds % 2]]:
            entry[2]()
            dense.remove(entry)
        rounds += 1
    for entry in dense:
        entry[2]()

    for j in range(N_A_BLOCKS):
        cols = slice(j * A_BLOCK, (j + 1) * A_BLOCK)
        pg = pgbuf[j]
        merged = _sigmoid(pg[:, :A_BLOCK]) * yabuf[:, cols] + _sigmoid(pg[:, A_BLOCK:]) * ybbuf[:, cols]
        mg_ref[j] = merged.astype(BF16)
    o_ref[...] = x + _dot(jnp.concatenate([mg_ref[j] for j in range(N_A_BLOCKS)], axis=1), wo_ref[...])


def _mixer_constants(tm):
    t = np.arange(tm)
    tril = ((t[:, None] >= t[None, :]) & (t[:, None] // CHUNK == t[None, :] // CHUNK))
    k = np.arange(LANES)[:, None]
    head = np.arange(D_INNER)[None, :] // SSM_HEAD_DIM
    expand = (k % SSM_HEADS == head) & (k < 2 * SSM_HEADS)
    r = np.arange(GROUP_WIDTH)
    bdmask = (r[:, None] // SSM_HEAD_DIM) == (r[None, :] // SSM_HEAD_DIM)
    return (jnp.asarray(tril, BF16), jnp.asarray(expand, BF16), jnp.asarray(bdmask, BF16))


def _mixer(x2d, batch, seq, gain, wmain, wdt, wg, conv_a, short_w_out, conv_b, conv_b_bias, dt_bias, a_log,
           d_skip, ssm_norm, ssm_w_out, w_out, *, tm=256):
    d = D_MODEL
    ns = seq // tm
    dtb = jnp.tile(dt_bias, HEAD_COPIES).reshape(1, LANES)
    alog = jnp.tile(a_log, HEAD_COPIES).reshape(1, LANES)
    dexp = jnp.repeat(d_skip, SSM_HEAD_DIM).reshape(1, D_INNER)
    tril, expand, bdmask = _mixer_constants(tm)
    operands = [
        gain.reshape(1, d), wmain, wdt, wg,
        conv_a, short_w_out, conv_b, conv_b_bias.reshape(1, D_XBC),
        dtb, alog, dexp,
        ssm_norm.reshape(1, D_INNER), ssm_w_out, w_out,
        tril, expand, bdmask,
    ]
    x_spec = pl.BlockSpec((tm, d), lambda b, s: (b * ns + s, 0))
    scratch = [
        pltpu.VMEM((tm, d), BF16),
        pltpu.VMEM((N_A_BLOCKS, SUBLANES + tm, A_BLOCK), F32),
        pltpu.VMEM((SSM_GROUPS, SUBLANES + tm, GROUP_XBC), F32),
        pltpu.VMEM((SSM_GROUPS, tm, GROUP_WIDTH), F32),
        pltpu.VMEM((SSM_GROUPS, D_STATE, GROUP_WIDTH), F32),
        pltpu.VMEM((3, tm // LANES, LANES, LANES), F32),
        pltpu.VMEM((N_A_BLOCKS, tm, A_BLOCK), BF16),
        pltpu.VMEM((SSM_GROUPS, tm, GROUP_WIDTH), BF16),
        pltpu.VMEM((N_A_BLOCKS, tm, A_BLOCK), BF16),
        pltpu.VMEM((N_A_BLOCKS, tm, 2 * A_BLOCK), F32),
        pltpu.VMEM((tm, d), F32),
        pltpu.VMEM((tm, d), F32),
    ]
    return pl.pallas_call(
        functools.partial(_mixer_body, tm=tm),
        grid=(batch, ns),
        in_specs=[x_spec] + [_resident(op.shape) for op in operands],
        out_specs=x_spec,
        out_shape=jax.ShapeDtypeStruct(x2d.shape, F32),
        scratch_shapes=scratch,
        compiler_params=pltpu.CompilerParams(
            dimension_semantics=("arbitrary", "arbitrary"), vmem_limit_bytes=VMEM_LIMIT_BYTES),
        name="mixer",
    )(x2d, *operands)


def kernel(x, ffn1_norm, ffn1_w_in, ffn1_w_out, mix_norm, w_in, short_conv_w, short_w_out, ssm_conv_w, ssm_conv_b, ssm_dt_bias, ssm_A_log, ssm_D, ssm_norm, ssm_w_out, w_out, ffn2_norm, ffn2_w_in, ffn2_w_out, final_norm):
    b, s, d = x.shape
    x2d = x.reshape(b * s, d)
    fin = final_norm.reshape(1, d)
    w_dt_start = W_XBC_START + D_XBC
    w_g_start = w_dt_start + SSM_HEADS
    for l in range(ffn1_norm.shape[0]):
        w_t = w_in[l].T
        wdt = jnp.tile(w_t[w_dt_start:w_g_start].T, (1, HEAD_COPIES)).astype(BF16)
        jobs = (
            _CastJob(w_t, (CAST_BLOCK, d), 0, True, count=w_dt_start),
            _CastJob(w_t, (CAST_BLOCK // 2, d), 0, True, start=w_g_start),
            _CastJob(ffn2_w_in[l], (d, CAST_BLOCK // 2), 1, False),
            _CastJob(ffn2_w_out[l], (CAST_BLOCK // 2, d), 0, False),
            _CastJob(short_w_out[l], (LANES, d), 0, False),
            _CastJob(ssm_w_out[l], (LANES, d), 0, False),
            _CastJob(w_out[l], (LANES, d), 0, False),
        )
        x2d, (wmain, wg, w2_in, w2_out, swo, sswo, wo) = _ffn(
            x2d, ffn1_norm[l].reshape(1, d), ffn1_w_in[l].astype(BF16), ffn1_w_out[l].astype(BF16), fin,
            final_norm=False, tm=512, jobs=jobs)
        x2d = _mixer(x2d, b, s, mix_norm[l], wmain, wdt, wg, short_conv_w[l], swo,
                     ssm_conv_w[l], ssm_conv_b[l], ssm_dt_bias[l], ssm_A_log[l], ssm_D[l],
                     ssm_norm[l], sswo, wo)
        last = l == ffn1_norm.shape[0] - 1
        x2d, _ = _ffn(x2d, ffn2_norm[l].reshape(1, d), w2_in, w2_out, fin, final_norm=last, tm=1024)
    return x2d.reshape(b, s, d)
```

```python
import functools
from typing import NamedTuple

import jax
import jax.numpy as jnp
import numpy as np
from jax import lax
from jax.experimental import pallas as pl
from jax.experimental.pallas import tpu as pltpu

D_MODEL = 1024
D_FF = 2816
D_SHORT = D_MODEL
SHORT_CONV = 3
D_INNER = 2048
SSM_HEADS = 32
SSM_HEAD_DIM = 64
SSM_GROUPS = 8
D_STATE = 128
SSM_CONV = 4
D_XBC = D_INNER + 2 * SSM_GROUPS * D_STATE
CHUNK = 64
NORM_EPS = 1e-5

HEADS_PER_GROUP = SSM_HEADS // SSM_GROUPS
GROUP_WIDTH = HEADS_PER_GROUP * SSM_HEAD_DIM
GROUP_XBC = GROUP_WIDTH + 2 * D_STATE
LANES = 128
SUBLANES = 8
HEAD_COPIES = LANES // SSM_HEADS
A_BLOCK = 256
N_A_BLOCKS = D_SHORT // A_BLOCK
GROUP_STAGGER = 3
DENSE_PIECES_PER_ROUND = (2, 1)
GROUPS_PROJECTED_AHEAD = 2
YB_GROUPS = 2
PITCH_ALIGN = 1024
PITCH_PAD = 512
CAST_BLOCK = 512
W_Z_START = 3 * D_SHORT
W_XBC_START = W_Z_START + D_INNER

F32 = jnp.float32
BF16 = jnp.bfloat16

VMEM_LIMIT_BYTES = 58 * 1024 * 1024
MASKED = -1e30
LOG2_E = 1.4426950408889634


def _rms_norm(x, gain):
    ms = jnp.mean(x * x, axis=-1, keepdims=True)
    return x * lax.rsqrt(ms + NORM_EPS) * gain


def _sigmoid(x):
    return 1.0 / (1.0 + jnp.exp(-x))


def _dot(a, b):
    return jnp.dot(a, b, preferred_element_type=F32)


def _split2(x):
    hi = x.astype(BF16).astype(F32)
    lo = (x - hi).astype(BF16).astype(F32)
    return hi, lo


def _resident(shape):
    nd = len(shape)
    return pl.BlockSpec(shape, lambda *_: (0,) * nd, pipeline_mode=pl.Buffered(1))


class _CastJob(NamedTuple):
    src: jax.Array
    block: tuple
    axis: int
    transpose: bool
    start: int = 0
    count: int = 0


    @property
    def extent(self):
        return self.count or self.src.shape[self.axis] - self.start

    @property
    def n_blocks(self):
        return self.extent // self.block[self.axis]

    @property
    def dst_core_shape(self):
        rows, cols = self.src.shape
        rows, cols = (self.extent, cols) if self.axis == 0 else (rows, self.extent)
        return (cols, rows) if self.transpose else (rows, cols)

    @property
    def dst_advances_along_columns(self):
        return (self.axis == 0) == self.transpose

    @property
    def pad_blocks(self):
        if not self.dst_advances_along_columns or self.dst_core_shape[1] % PITCH_ALIGN:
            return 0
        return PITCH_PAD // self.block[self.axis]

    @property
    def pad_columns(self):
        if self.dst_advances_along_columns or self.dst_core_shape[1] % PITCH_ALIGN:
            return 0
        return PITCH_PAD

    def specs(self):
        n, axis, block = self.n_blocks, self.axis, self.block
        step = block[axis]

        def src_index(i):
            k = jnp.minimum(i, n - 1)
            if self.start:
                assert self.start % SUBLANES == 0 and step % SUBLANES == 0
                return (pl.multiple_of(self.start + k * step, SUBLANES), 0)
            return (k, 0) if axis == 0 else (0, k)

        def dst_index(i):
            k = jnp.minimum(i, n - 1 + self.pad_blocks)
            return (0, k) if self.dst_advances_along_columns else (k, 0)

        src_block = (pl.Element(block[0]), pl.Element(block[1])) if self.start else block
        dst_block = block[::-1] if self.transpose else block
        dst_block = (dst_block[0], dst_block[1] + self.pad_columns)
        rows, cols = self.dst_core_shape
        cols += self.pad_columns + self.pad_blocks * step
        return (pl.BlockSpec(src_block, src_index), pl.BlockSpec(dst_block, dst_index),
                jax.ShapeDtypeStruct((rows, cols), BF16))


def _ffn_body(x_ref, gain_ref, w_in_ref, w_out_ref, fin_ref, *rest, ff_chunk, final_norm, casts):
    cast_src, o_ref, cast_dst = rest[:len(casts)], rest[len(casts)], rest[len(casts) + 1:]
    pending = list(zip(casts, cast_src, cast_dst))

    x = x_ref[...]
    h = _rms_norm(x, gain_ref[...]).astype(BF16)
    acc = jnp.zeros(x.shape, F32)
    n_chunks = D_FF // ff_chunk
    for j in range(n_chunks):
        lo = j * ff_chunk
        g = _dot(h, w_in_ref[:, lo:lo + ff_chunk])
        u = _dot(h, w_in_ref[:, D_FF + lo:D_FF + lo + ff_chunk])
        a = (g * _sigmoid(g) * u).astype(BF16)
        acc = acc + _dot(a, w_out_ref[lo:lo + ff_chunk, :D_MODEL])
        for (transpose, pad), src_ref, dst_ref in (pending[j:j + 1] if j < n_chunks - 1 else pending[j:]):
            v = src_ref[...]
            v = (v.T if transpose else v).astype(BF16)
            if pad:
                dst_ref[:, :v.shape[1]] = v
                dst_ref[:, v.shape[1]:] = jnp.zeros((v.shape[0], pad), BF16)
            else:
                dst_ref[...] = v
    y = x + 0.5 * acc
    if final_norm:
        y = _rms_norm(y, fin_ref[...])
    o_ref[...] = y


def _ffn(x2d, gain, w_in, w_out, fin, *, final_norm, tm, ff_chunk=256, jobs=()):
    t, d = x2d.shape
    n_steps = t // tm
    assert all(job.n_blocks <= n_steps for job in jobs)
    job_specs = [job.specs() for job in jobs]
    body = functools.partial(_ffn_body, ff_chunk=ff_chunk, final_norm=final_norm,
                             casts=tuple((job.transpose, job.pad_columns) for job in jobs))
    x_spec = pl.BlockSpec((tm, d), lambda i: (i, 0))
    outs = pl.pallas_call(
        body,
        grid=(n_steps,),
        in_specs=[x_spec, _resident((1, d)), _resident(w_in.shape), _resident(w_out.shape), _resident((1, d))]
        + [s[0] for s in job_specs],
        out_specs=[x_spec] + [s[1] for s in job_specs],
        out_shape=[jax.ShapeDtypeStruct((t, d), F32)] + [s[2] for s in job_specs],
        compiler_params=pltpu.CompilerParams(
            dimension_semantics=("arbitrary",), vmem_limit_bytes=VMEM_LIMIT_BYTES),
        name="ffn_final" if final_norm else "ffn",
    )(x2d, gain, w_in, w_out, fin, *[job.src for job in jobs])
    return outs[0], outs[1:]


def _mixer_body(x_ref, gain_ref, wmain_ref, wdt_ref, wg_ref,
                cwa_ref, swo_ref, cwb_ref, cbb_ref, dtb_ref, alog_ref, dexp_ref,
                snorm_ref, sswo_ref, wo_ref, tril_ref, expand_ref, bdmask_ref,
                o_ref,
                hb_ref, ubuf, xbuf, zbuf, st_ref, tr_ref, yap_ref, yn_ref, mg_ref, pgbuf, yabuf, ybbuf,
                *, tm):
    n_chunks = tm // CHUNK

    @pl.when(pl.program_id(1) == 0)
    def _():
        ubuf[:, 0:SUBLANES, :] = jnp.zeros((N_A_BLOCKS, SUBLANES, A_BLOCK), F32)
        xbuf[:, 0:SUBLANES, :] = jnp.zeros((SSM_GROUPS, SUBLANES, GROUP_XBC), F32)
        st_ref[...] = jnp.zeros(st_ref.shape, F32)

    x = x_ref[...]
    hb_ref[...] = _rms_norm(x, gain_ref[...]).astype(BF16)

    def project(w_ref, pieces):
        w = [w_ref[:, lo:lo + width] for lo, width in pieces]
        return _dot(hb_ref[...], w[0] if len(w) == 1 else jnp.concatenate(w, axis=1))

    def xbc_pieces(g):
        return [(g * GROUP_WIDTH, GROUP_WIDTH),
                (D_INNER + g * D_STATE, D_STATE),
                (D_INNER + (SSM_GROUPS + g) * D_STATE, D_STATE)]

    def gather_xbc(ref, rows, g):
        return jnp.concatenate([ref[rows, lo:lo + width] for lo, width in xbc_pieces(g)], axis=1)

    pre = _dot(hb_ref[...], wdt_ref[...]) + dtb_ref[...]
    dt = jnp.maximum(pre, 0.0) + jnp.log1p(jnp.exp(-jnp.abs(pre)))
    a = dt * (-jnp.exp(alog_ref[...]))
    a_hi = a.astype(BF16)
    a_r1 = a - a_hi.astype(F32)
    a_mid = a_r1.astype(BF16)
    a_lo = (a_r1 - a_mid.astype(F32)).astype(BF16)
    tril = tril_ref[...]
    cs = (_dot(tril, a_hi) + _dot(tril, a_mid) + _dot(tril, a_lo)) * LOG2_E
    cs_last = jnp.concatenate(
        [jnp.broadcast_to(cs[(c + 1) * CHUNK - 1:(c + 1) * CHUNK, :], (CHUNK, LANES)) for c in range(n_chunks)], axis=0)
    to_end = dt * jnp.exp2(cs_last - cs)
    cs_hi, cs_lo = _split2(cs)
    lane_t = lax.broadcasted_iota(jnp.int32, (tm, LANES), 1)
    packed = jnp.where(lane_t < SSM_HEADS, cs_hi, jnp.where(lane_t < 2 * SSM_HEADS, cs_lo, 0.0)).astype(BF16)

    lane_8 = lax.broadcasted_iota(jnp.int32, (SSM_GROUPS, LANES), 1)

    def along_lanes(v, slot):
        v_t = v.T
        for blk in range(tm // LANES):
            tr_ref[slot, blk] = v_t[:, blk * LANES:(blk + 1) * LANES]
        out = []
        for c in range(n_chunks):
            blk = c // 2
            pieces = [tr_ref[slot, blk, pl.ds(hh, SSM_GROUPS, stride=HEADS_PER_GROUP), :]
                      for hh in range(HEADS_PER_GROUP)]
            rolled = [pltpu.roll(p, CHUNK, axis=1) for p in pieces]
            if c % 2 == 0:
                halves = [jnp.where(lane_8 < CHUNK, pieces[0], rolled[1]),
                          jnp.where(lane_8 < CHUNK, pieces[2], rolled[3])]
            else:
                halves = [jnp.where(lane_8 < CHUNK, rolled[0], pieces[1]),
                          jnp.where(lane_8 < CHUNK, rolled[2], pieces[3])]
            out.append(jnp.concatenate(halves, axis=1))
        return out

    cs_rows = along_lanes(cs, 0)
    dt_rows = along_lanes(dt, 1)
    to_end_rows = along_lanes(to_end, 2)

    a_proj = {}

    def project_a(j):
        a_proj[j] = project(wmain_ref, [(part * D_SHORT + j * A_BLOCK, A_BLOCK) for part in range(3)])

    def project_xbc(g):
        xbuf[g, SUBLANES:SUBLANES + tm, :] = project(
            wmain_ref, [(W_XBC_START + lo, width) for lo, width in xbc_pieces(g)])

    def project_z(g):
        zbuf[g] = project(wmain_ref, [(W_Z_START + g * GROUP_WIDTH, GROUP_WIDTH)])

    def conv_a(j):
        cols = slice(j * A_BLOCK, (j + 1) * A_BLOCK)
        p = a_proj.pop(j)
        u = p[:, A_BLOCK:2 * A_BLOCK] * p[:, 2 * A_BLOCK:]
        ubuf[j, SUBLANES:SUBLANES + tm, :] = u
        va = cwa_ref[SHORT_CONV - 1:SHORT_CONV, cols] * u
        for k in range(SHORT_CONV - 1):
            off = SUBLANES - (SHORT_CONV - 1) + k
            va = va + cwa_ref[k:k + 1, cols] * ubuf[j, off:off + tm, :]
        ubuf[j, 0:SUBLANES, :] = ubuf[j, tm:tm + SUBLANES, :]
        yap_ref[j] = (p[:, :A_BLOCK] * va).astype(BF16)

    lane_sq = lax.broadcasted_iota(jnp.int32, (LANES, LANES), 1)
    row_i = lax.broadcasted_iota(jnp.int32, (CHUNK, GROUP_WIDTH), 0)
    lane_j = lax.broadcasted_iota(jnp.int32, (CHUNK, GROUP_WIDTH), 1) % CHUNK
    causal = row_i >= lane_j
    bdmask = bdmask_ref[...]

    def group_stages(g):
        gcols = slice(g * GROUP_WIDTH, (g + 1) * GROUP_WIDTH)
        acc = (gather_xbc(cbb_ref, slice(None), g)
               + gather_xbc(cwb_ref, slice(SSM_CONV - 1, SSM_CONV), g) * xbuf[g, SUBLANES:SUBLANES + tm, :])
        for k in range(SSM_CONV - 1):
            off = SUBLANES - (SSM_CONV - 1) + k
            acc = acc + gather_xbc(cwb_ref, slice(k, k + 1), g) * xbuf[g, off:off + tm, :]
        xbuf[g, 0:SUBLANES, :] = xbuf[g, tm:tm + SUBLANES, :]
        v = acc * _sigmoid(acc)
        yield
        xs = v[:, :GROUP_WIDTH]
        xs_bf = xs.astype(BF16)
        cmg = v[:, GROUP_WIDTH + D_STATE:].astype(BF16)
        bm_t = v[:, GROUP_WIDTH:GROUP_WIDTH + D_STATE].T
        bt = []
        for blk in range(tm // LANES):
            w = bm_t[:, blk * LANES:(blk + 1) * LANES]
            r = pltpu.roll(w, CHUNK, axis=1)
            bt.append(jnp.where(lane_sq < CHUNK, w, r))
            bt.append(jnp.where(lane_sq < CHUNK, r, w))
        cscol = _dot(packed, expand_ref[:, gcols])
        yield

        state = st_ref[g]
        ys = []
        for c in range(n_chunks):
            rows = slice(c * CHUNK, (c + 1) * CHUNK)
            col = cscol[rows]
            cm_c = cmg[rows]
            bt4 = jnp.concatenate([bt[c], bt[c]], axis=1)
            cb4 = _dot(cm_c, bt4.astype(BF16))
            decay = jnp.exp2(jnp.where(causal, col - cs_rows[c][g:g + 1, :], MASKED))
            m = (cb4 * decay * dt_rows[c][g:g + 1, :]).astype(BF16)
            b_end = (bt4 * to_end_rows[c][g:g + 1, :]).astype(BF16)
            bd = jnp.concatenate([xs_bf[rows]] * HEADS_PER_GROUP, axis=0) * bdmask
            both = _dot(jnp.concatenate([m, b_end], axis=0), bd)
            y_off = _dot(cm_c, state.astype(BF16)) * jnp.exp2(col)
            ys.append(both[:CHUNK] + y_off + xs[rows] * dexp_ref[:, gcols])
            state = state * jnp.exp2(col[CHUNK - 1:CHUNK, :]) + both[CHUNK:]
            yield
        st_ref[g] = state
        y = jnp.concatenate(ys, axis=0)

        z = zbuf[g]
        yz = y * (z * _sigmoid(z))
        yn = yz * lax.rsqrt(jnp.mean(yz * yz, axis=-1, keepdims=True) + NORM_EPS) * snorm_ref[:, gcols]
        yn_ref[g] = yn.astype(BF16)

    def ssd_out_partial(g_lo, g_hi):
        yn = jnp.concatenate([yn_ref[g] for g in range(g_lo, g_hi)], axis=1)
        return _dot(yn, sswo_ref[g_lo * GROUP_WIDTH:g_hi * GROUP_WIDTH, :D_MODEL])

    def project_gates(j):
        pgbuf[j] = project(wg_ref, [(part * D_MODEL + j * A_BLOCK, A_BLOCK) for part in range(2)])

    def project_ya():
        yabuf[...] = _dot(jnp.concatenate([yap_ref[j] for j in range(N_A_BLOCKS)], axis=1), swo_ref[:, :D_MODEL])

    def accumulate_yb(g_lo, g_hi):
        if g_lo == 0:
            ybbuf[...] = ssd_out_partial(g_lo, g_hi)
        else:
            ybbuf[...] += ssd_out_partial(g_lo, g_hi)

    fillers = []
    for j in range(N_A_BLOCKS):
        fillers += [functools.partial(project_a, j), functools.partial(conv_a, j)]
    fillers += [project_ya] + [functools.partial(project_gates, j) for j in range(N_A_BLOCKS)]
    for g in range(GROUPS_PROJECTED_AHEAD):
        project_xbc(g)
        project_z(g)
    dense = []
    for g in range(GROUPS_PROJECTED_AHEAD, SSM_GROUPS):
        dense.append((0, g, functools.partial(project_xbc, g)))
        dense.append((0, g, functools.partial(project_z, g)))
        if fillers:
            dense.append((0, None, fillers.pop(0)))
    dense += [(0, None, f) for f in fillers]
    for g in range(0, SSM_GROUPS, YB_GROUPS):
        dense.append((g + YB_GROUPS, None, functools.partial(accumulate_yb, g, g + YB_GROUPS)))

    live, started, finished, rounds = [], 0, 0, 0
    while started < SSM_GROUPS or live:
        if started < SSM_GROUPS and rounds % GROUP_STAGGER == 0:
            for entry in [e for e in dense if e[1] == started]:
                entry[2]()
                dense.remove(entry)
            live.append(group_stages(started))
            started += 1
        still_live = [stages for stages in live if next(stages, True) is None]
        finished += len(live) - len(still_live)
        live = still_live
        for entry in [e for e in dense if e[0] <= finished][:DENSE_PIECES_PER_ROUND[rounds % 2]]:
            entry[2]()
            dense.remove(entry)
        rounds += 1
    for entry in dense:
        entry[2]()

    for j in range(N_A_BLOCKS):
        cols = slice(j * A_BLOCK, (j + 1) * A_BLOCK)
        pg = pgbuf[j]
        merged = _sigmoid(pg[:, :A_BLOCK]) * yabuf[:, cols] + _sigmoid(pg[:, A_BLOCK:]) * ybbuf[:, cols]
        mg_ref[j] = merged.astype(BF16)
    o_ref[...] = x + _dot(jnp.concatenate([mg_ref[j] for j in range(N_A_BLOCKS)], axis=1), wo_ref[:, :D_MODEL])


def _mixer_constants(tm):
    t = np.arange(tm)
    tril = ((t[:, None] >= t[None, :]) & (t[:, None] // CHUNK == t[None, :] // CHUNK))
    k = np.arange(LANES)[:, None]
    head = np.arange(D_INNER)[None, :] // SSM_HEAD_DIM
    expand = (k % SSM_HEADS == head) & (k < 2 * SSM_HEADS)
    r = np.arange(GROUP_WIDTH)
    bdmask = (r[:, None] // SSM_HEAD_DIM) == (r[None, :] // SSM_HEAD_DIM)
    return (jnp.asarray(tril, BF16), jnp.asarray(expand, BF16), jnp.asarray(bdmask, BF16))


def _mixer(x2d, batch, seq, gain, wmain, wdt, wg, conv_a, short_w_out, conv_b, conv_b_bias, dt_bias, a_log,
           d_skip, ssm_norm, ssm_w_out, w_out, *, tm=256):
    d = D_MODEL
    ns = seq // tm
    dtb = jnp.tile(dt_bias, HEAD_COPIES).reshape(1, LANES)
    alog = jnp.tile(a_log, HEAD_COPIES).reshape(1, LANES)
    dexp = jnp.repeat(d_skip, SSM_HEAD_DIM).reshape(1, D_INNER)
    tril, expand, bdmask = _mixer_constants(tm)
    operands = [
        gain.reshape(1, d), wmain, wdt, wg,
        conv_a, short_w_out, conv_b, conv_b_bias.reshape(1, D_XBC),
        dtb, alog, dexp,
        ssm_norm.reshape(1, D_INNER), ssm_w_out, w_out,
        tril, expand, bdmask,
    ]
    x_spec = pl.BlockSpec((tm, d), lambda b, s: (b * ns + s, 0))
    scratch = [
        pltpu.VMEM((tm, d), BF16),
        pltpu.VMEM((N_A_BLOCKS, SUBLANES + tm, A_BLOCK), F32),
        pltpu.VMEM((SSM_GROUPS, SUBLANES + tm, GROUP_XBC), F32),
        pltpu.VMEM((SSM_GROUPS, tm, GROUP_WIDTH), F32),
        pltpu.VMEM((SSM_GROUPS, D_STATE, GROUP_WIDTH), F32),
        pltpu.VMEM((3, tm // LANES, LANES, LANES), F32),
        pltpu.VMEM((N_A_BLOCKS, tm, A_BLOCK), BF16),
        pltpu.VMEM((SSM_GROUPS, tm, GROUP_WIDTH), BF16),
        pltpu.VMEM((N_A_BLOCKS, tm, A_BLOCK), BF16),
        pltpu.VMEM((N_A_BLOCKS, tm, 2 * A_BLOCK), F32),
        pltpu.VMEM((tm, d), F32),
        pltpu.VMEM((tm, d), F32),
    ]
    return pl.pallas_call(
        functools.partial(_mixer_body, tm=tm),
        grid=(batch, ns),
        in_specs=[x_spec] + [_resident(op.shape) for op in operands],
        out_specs=x_spec,
        out_shape=jax.ShapeDtypeStruct(x2d.shape, F32),
        scratch_shapes=scratch,
        compiler_params=pltpu.CompilerParams(
            dimension_semantics=("arbitrary", "arbitrary"), vmem_limit_bytes=VMEM_LIMIT_BYTES),
        name="mixer",
    )(x2d, *operands)


def kernel(x, ffn1_norm, ffn1_w_in, ffn1_w_out, mix_norm, w_in, short_conv_w, short_w_out, ssm_conv_w, ssm_conv_b, ssm_dt_bias, ssm_A_log, ssm_D, ssm_norm, ssm_w_out, w_out, ffn2_norm, ffn2_w_in, ffn2_w_out, final_norm):
    b, s, d = x.shape
    x2d = x.reshape(b * s, d)
    fin = final_norm.reshape(1, d)
    w_dt_start = W_XBC_START + D_XBC
    w_g_start = w_dt_start + SSM_HEADS
    for l in range(ffn1_norm.shape[0]):
        w_t = w_in[l].T
        wdt = jnp.tile(w_t[w_dt_start:w_g_start].T, (1, HEAD_COPIES)).astype(BF16)
        jobs = (
            _CastJob(w_t, (CAST_BLOCK, d), 0, True, count=w_dt_start),
            _CastJob(w_t, (CAST_BLOCK // 2, d), 0, True, start=w_g_start),
            _CastJob(ffn2_w_in[l], (d, CAST_BLOCK // 2), 1, False),
            _CastJob(ffn2_w_out[l], (CAST_BLOCK // 2, d), 0, False),
            _CastJob(short_w_out[l], (LANES, d), 0, False),
            _CastJob(ssm_w_out[l], (LANES, d), 0, False),
            _CastJob(w_out[l], (LANES, d), 0, False),
        )
        x2d, (wmain, wg, w2_in, w2_out, swo, sswo, wo) = _ffn(
            x2d, ffn1_norm[l].reshape(1, d), ffn1_w_in[l].astype(BF16),
            jnp.pad(ffn1_w_out[l].astype(BF16), ((0, 0), (0, PITCH_PAD))), fin,
            final_norm=False, tm=512, jobs=jobs)
        x2d = _mixer(x2d, b, s, mix_norm[l], wmain, wdt, wg, short_conv_w[l], swo,
                     ssm_conv_w[l], ssm_conv_b[l], ssm_dt_bias[l], ssm_A_log[l], ssm_D[l],
                     ssm_norm[l], sswo, wo)
        last = l == ffn1_norm.shape[0] - 1
        x2d, _ = _ffn(x2d, ffn2_norm[l].reshape(1, d), w2_in, w2_out, fin, final_norm=last, tm=1024)
    return x2d.reshape(b, s, d)
```

```python
import functools
from typing import NamedTuple

import jax
import jax.numpy as jnp
import numpy as np
from jax import lax
from jax.experimental import pallas as pl
from jax.experimental.pallas import tpu as pltpu

D_MODEL = 1024
D_FF = 2816
D_SHORT = D_MODEL
SHORT_CONV = 3
D_INNER = 2048
SSM_HEADS = 32
SSM_HEAD_DIM = 64
SSM_GROUPS = 8
D_STATE = 128
SSM_CONV = 4
D_XBC = D_INNER + 2 * SSM_GROUPS * D_STATE
CHUNK = 64
NORM_EPS = 1e-5

HEADS_PER_GROUP = SSM_HEADS // SSM_GROUPS
GROUP_WIDTH = HEADS_PER_GROUP * SSM_HEAD_DIM
GROUP_XBC = GROUP_WIDTH + 2 * D_STATE
LANES = 128
SUBLANES = 8
HEAD_COPIES = LANES // SSM_HEADS
A_BLOCK = 256
N_A_BLOCKS = D_SHORT // A_BLOCK
GROUP_STAGGER = 3
DENSE_PIECES_PER_ROUND = (2, 1)
GROUPS_PROJECTED_AHEAD = 2
YB_GROUPS = 2
PITCH_ALIGN = 1024
PITCH_PAD = 512
CAST_BLOCK = 512
W_Z_START = 3 * D_SHORT
W_XBC_START = W_Z_START + D_INNER

F32 = jnp.float32
BF16 = jnp.bfloat16

VMEM_LIMIT_BYTES = 58 * 1024 * 1024
MASKED = -1e30
LOG2_E = 1.4426950408889634


def _rms_norm(x, gain):
    ms = jnp.mean(x * x, axis=-1, keepdims=True)
    return x * lax.rsqrt(ms + NORM_EPS) * gain


def _sigmoid(x):
    return 1.0 / (1.0 + jnp.exp(-x))


def _dot(a, b):
    return jnp.dot(a, b, preferred_element_type=F32)


def _split3(x):
    hi = x.astype(BF16).astype(F32)
    mid = (x - hi).astype(BF16).astype(F32)
    lo = (x - hi - mid).astype(BF16).astype(F32)
    return hi, mid, lo


def _resident(shape):
    nd = len(shape)
    return pl.BlockSpec(shape, lambda *_: (0,) * nd, pipeline_mode=pl.Buffered(1))


class _CastJob(NamedTuple):
    src: jax.Array
    block: tuple
    axis: int
    transpose: bool
    start: int = 0
    count: int = 0


    @property
    def extent(self):
        return self.count or self.src.shape[self.axis] - self.start

    @property
    def n_blocks(self):
        return self.extent // self.block[self.axis]

    @property
    def dst_core_shape(self):
        rows, cols = self.src.shape
        rows, cols = (self.extent, cols) if self.axis == 0 else (rows, self.extent)
        return (cols, rows) if self.transpose else (rows, cols)

    @property
    def dst_advances_along_columns(self):
        return (self.axis == 0) == self.transpose

    @property
    def pad_blocks(self):
        if not self.dst_advances_along_columns or self.dst_core_shape[1] % PITCH_ALIGN:
            return 0
        return PITCH_PAD // self.block[self.axis]

    @property
    def pad_columns(self):
        if self.dst_advances_along_columns or self.dst_core_shape[1] % PITCH_ALIGN:
            return 0
        return PITCH_PAD

    def specs(self):
        n, axis, block = self.n_blocks, self.axis, self.block
        step = block[axis]

        def src_index(i):
            k = jnp.minimum(i, n - 1)
            if self.start:
                assert self.start % SUBLANES == 0 and step % SUBLANES == 0
                return (pl.multiple_of(self.start + k * step, SUBLANES), 0)
            return (k, 0) if axis == 0 else (0, k)

        def dst_index(i):
            k = jnp.minimum(i, n - 1 + self.pad_blocks)
            return (0, k) if self.dst_advances_along_columns else (k, 0)

        src_block = (pl.Element(block[0]), pl.Element(block[1])) if self.start else block
        dst_block = block[::-1] if self.transpose else block
        dst_block = (dst_block[0], dst_block[1] + self.pad_columns)
        rows, cols = self.dst_core_shape
        cols += self.pad_columns + self.pad_blocks * step
        return (pl.BlockSpec(src_block, src_index), pl.BlockSpec(dst_block, dst_index),
                jax.ShapeDtypeStruct((rows, cols), BF16))


def _ffn_body(x_ref, gain_ref, w_in_ref, w_out_ref, fin_ref, *rest, ff_chunk, final_norm, casts, row_parts):
    cast_src, o_ref, cast_dst = rest[:len(casts)], rest[len(casts)], rest[len(casts) + 1:]
    pending = list(zip(casts, cast_src, cast_dst))

    part_rows = x_ref.shape[0] // row_parts
    n_chunks = D_FF // ff_chunk
    for part in range(row_parts):
        rows = slice(part * part_rows, (part + 1) * part_rows)
        x = x_ref[rows, :]
        h = _rms_norm(x, gain_ref[...]).astype(BF16)
        acc = jnp.zeros(x.shape, F32)
        for j in range(n_chunks):
            lo = j * ff_chunk
            g = _dot(h, w_in_ref[:, lo:lo + ff_chunk])
            u = _dot(h, w_in_ref[:, D_FF + lo:D_FF + lo + ff_chunk])
            a = (g * _sigmoid(g) * u).astype(BF16)
            acc = acc + _dot(a, w_out_ref[lo:lo + ff_chunk, :D_MODEL])
            if part == 0:
                for (transpose, pad), src_ref, dst_ref in (pending[j:j + 1] if j < n_chunks - 1 else pending[j:]):
                    v = src_ref[...]
                    v = v.astype(BF16).T if transpose else v.astype(BF16)
                    if pad:
                        dst_ref[:, :v.shape[1]] = v
                        dst_ref[:, v.shape[1]:] = jnp.zeros((v.shape[0], pad), BF16)
                    else:
                        dst_ref[...] = v
        y = x + 0.5 * acc
        if final_norm:
            y = _rms_norm(y, fin_ref[...])
        o_ref[rows, :] = y


def _ffn(x2d, gain, w_in, w_out, fin, *, final_norm, tm, ff_chunk=256, jobs=(), row_parts=1):
    t, d = x2d.shape
    n_steps = t // tm
    assert all(job.n_blocks <= n_steps for job in jobs)
    job_specs = [job.specs() for job in jobs]
    body = functools.partial(_ffn_body, ff_chunk=ff_chunk, final_norm=final_norm, row_parts=row_parts,
                             casts=tuple((job.transpose, job.pad_columns) for job in jobs))
    x_spec = pl.BlockSpec((tm, d), lambda i: (i, 0))
    outs = pl.pallas_call(
        body,
        grid=(n_steps,),
        in_specs=[x_spec, _resident((1, d)), _resident(w_in.shape), _resident(w_out.shape), _resident((1, d))]
        + [s[0] for s in job_specs],
        out_specs=[x_spec] + [s[1] for s in job_specs],
        out_shape=[jax.ShapeDtypeStruct((t, d), F32)] + [s[2] for s in job_specs],
        compiler_params=pltpu.CompilerParams(
            dimension_semantics=("arbitrary",), vmem_limit_bytes=VMEM_LIMIT_BYTES),
        name="ffn_final" if final_norm else "ffn",
    )(x2d, gain, w_in, w_out, fin, *[job.src for job in jobs])
    return outs[0], outs[1:]


def _mixer_body(x_ref, gain_ref, wmain_ref, wdt_ref, wg_ref,
                cwa_ref, swo_ref, cwb_ref, cbb_ref, dtb_ref, alog_ref, dexp_ref,
                snorm_ref, sswo_ref, wo_ref, tril_ref, expand_ref, bdmask_ref,
                o_ref,
                hb_ref, ubuf, xbuf, zbuf, st_ref, tr_ref, yap_ref, yn_ref, mg_ref, pgbuf, yabuf, ybbuf,
                *, tm):
    n_chunks = tm // CHUNK

    @pl.when(pl.program_id(1) == 0)
    def _():
        ubuf[:, 0:SUBLANES, :] = jnp.zeros((N_A_BLOCKS, SUBLANES, A_BLOCK), F32)
        xbuf[:, 0:SUBLANES, :] = jnp.zeros((SSM_GROUPS, SUBLANES, GROUP_XBC), F32)
        st_ref[...] = jnp.zeros(st_ref.shape, F32)

    x = x_ref[...]
    hb_ref[...] = _rms_norm(x, gain_ref[...]).astype(BF16)

    def project(w_ref, pieces):
        w = [w_ref[:, lo:lo + width] for lo, width in pieces]
        return _dot(hb_ref[...], w[0] if len(w) == 1 else jnp.concatenate(w, axis=1))

    def xbc_pieces(g):
        return [(g * GROUP_WIDTH, GROUP_WIDTH),
                (D_INNER + g * D_STATE, D_STATE),
                (D_INNER + (SSM_GROUPS + g) * D_STATE, D_STATE)]

    def gather_xbc(ref, rows, g):
        return jnp.concatenate([ref[rows, lo:lo + width] for lo, width in xbc_pieces(g)], axis=1)

    pre = _dot(hb_ref[...], wdt_ref[...]) + dtb_ref[...]
    dt = jnp.maximum(pre, 0.0) + jnp.log1p(jnp.exp(-jnp.abs(pre)))
    a = dt * (-jnp.exp(alog_ref[...]))
    a_hi, a_mid, a_lo = (part.astype(BF16) for part in _split3(a))
    tril = tril_ref[...]
    cs = (_dot(tril, a_hi) + _dot(tril, a_mid) + _dot(tril, a_lo)) * LOG2_E
    cs_last = jnp.concatenate(
        [jnp.broadcast_to(cs[(c + 1) * CHUNK - 1:(c + 1) * CHUNK, :], (CHUNK, LANES)) for c in range(n_chunks)], axis=0)
    to_end = dt * jnp.exp2(cs_last - cs)
    cs_hi, cs_mid, cs_lo = _split3(cs)
    lane_t = lax.broadcasted_iota(jnp.int32, (tm, LANES), 1)
    packed = jnp.where(lane_t < SSM_HEADS, cs_hi,
                       jnp.where(lane_t < 2 * SSM_HEADS, cs_mid,
                                 jnp.where(lane_t < 3 * SSM_HEADS, cs_lo, 0.0))).astype(BF16)

    lane_8 = lax.broadcasted_iota(jnp.int32, (SSM_GROUPS, LANES), 1)

    def along_lanes(v, slot):
        v_t = v.T
        for blk in range(tm // LANES):
            tr_ref[slot, blk] = v_t[:, blk * LANES:(blk + 1) * LANES]
        out = []
        for c in range(n_chunks):
            blk = c // 2
            pieces = [tr_ref[slot, blk, pl.ds(hh, SSM_GROUPS, stride=HEADS_PER_GROUP), :]
                      for hh in range(HEADS_PER_GROUP)]
            rolled = [pltpu.roll(p, CHUNK, axis=1) for p in pieces]
            if c % 2 == 0:
                halves = [jnp.where(lane_8 < CHUNK, pieces[0], rolled[1]),
                          jnp.where(lane_8 < CHUNK, pieces[2], rolled[3])]
            else:
                halves = [jnp.where(lane_8 < CHUNK, rolled[0], pieces[1]),
                          jnp.where(lane_8 < CHUNK, rolled[2], pieces[3])]
            out.append(jnp.concatenate(halves, axis=1))
        return out

    cs_rows = along_lanes(cs, 0)
    dt_rows = along_lanes(dt, 1)
    to_end_rows = along_lanes(to_end, 2)

    a_proj = {}

    def project_a(j):
        a_proj[j] = project(wmain_ref, [(part * D_SHORT + j * A_BLOCK, A_BLOCK) for part in range(3)])

    def project_xbc(g):
        xbuf[g, SUBLANES:SUBLANES + tm, :] = project(
            wmain_ref, [(W_XBC_START + lo, width) for lo, width in xbc_pieces(g)])

    def project_z(g):
        zbuf[g] = project(wmain_ref, [(W_Z_START + g * GROUP_WIDTH, GROUP_WIDTH)])

    def conv_a(j):
        cols = slice(j * A_BLOCK, (j + 1) * A_BLOCK)
        p = a_proj.pop(j)
        u = p[:, A_BLOCK:2 * A_BLOCK] * p[:, 2 * A_BLOCK:]
        ubuf[j, SUBLANES:SUBLANES + tm, :] = u
        va = cwa_ref[SHORT_CONV - 1:SHORT_CONV, cols] * u
        for k in range(SHORT_CONV - 1):
            off = SUBLANES - (SHORT_CONV - 1) + k
            va = va + cwa_ref[k:k + 1, cols] * ubuf[j, off:off + tm, :]
        ubuf[j, 0:SUBLANES, :] = ubuf[j, tm:tm + SUBLANES, :]
        yap_ref[j] = (p[:, :A_BLOCK] * va).astype(BF16)

    lane_sq = lax.broadcasted_iota(jnp.int32, (LANES, LANES), 1)
    row_i = lax.broadcasted_iota(jnp.int32, (CHUNK, GROUP_WIDTH), 0)
    lane_j = lax.broadcasted_iota(jnp.int32, (CHUNK, GROUP_WIDTH), 1) % CHUNK
    causal = row_i >= lane_j
    bdmask = bdmask_ref[...]

    def group_stages(g):
        gcols = slice(g * GROUP_WIDTH, (g + 1) * GROUP_WIDTH)
        acc = (gather_xbc(cbb_ref, slice(None), g)
               + gather_xbc(cwb_ref, slice(SSM_CONV - 1, SSM_CONV), g) * xbuf[g, SUBLANES:SUBLANES + tm, :])
        for k in range(SSM_CONV - 1):
            off = SUBLANES - (SSM_CONV - 1) + k
            acc = acc + gather_xbc(cwb_ref, slice(k, k + 1), g) * xbuf[g, off:off + tm, :]
        xbuf[g, 0:SUBLANES, :] = xbuf[g, tm:tm + SUBLANES, :]
        v = acc * _sigmoid(acc)
        yield
        xs = v[:, :GROUP_WIDTH]
        xs_bf = xs.astype(BF16)
        cmg = v[:, GROUP_WIDTH + D_STATE:].astype(BF16)
        bm_t = v[:, GROUP_WIDTH:GROUP_WIDTH + D_STATE].T
        bt = []
        for blk in range(tm // LANES):
            w = bm_t[:, blk * LANES:(blk + 1) * LANES]
            r = pltpu.roll(w, CHUNK, axis=1)
            bt.append(jnp.where(lane_sq < CHUNK, w, r))
            bt.append(jnp.where(lane_sq < CHUNK, r, w))
        cscol = _dot(packed, expand_ref[:, gcols])
        yield

        state = st_ref[g]
        ys = []
        for c in range(n_chunks):
            rows = slice(c * CHUNK, (c + 1) * CHUNK)
            col = cscol[rows]
            cm_c = cmg[rows]
            bt4 = jnp.concatenate([bt[c], bt[c]], axis=1)
            cb4 = _dot(cm_c, bt4.astype(BF16))
            decay = jnp.exp2(jnp.where(causal, col - cs_rows[c][g:g + 1, :], MASKED))
            m = (cb4 * decay * dt_rows[c][g:g + 1, :]).astype(BF16)
            b_end = (bt4 * to_end_rows[c][g:g + 1, :]).astype(BF16)
            bd = jnp.concatenate([xs_bf[rows]] * HEADS_PER_GROUP, axis=0) * bdmask
            both = _dot(jnp.concatenate([m, b_end], axis=0), bd)
            y_off = _dot(cm_c, state.astype(BF16)) * jnp.exp2(col)
            ys.append(both[:CHUNK] + y_off + xs[rows] * dexp_ref[:, gcols])
            state = state * jnp.exp2(col[CHUNK - 1:CHUNK, :]) + both[CHUNK:]
            yield
        st_ref[g] = state
        y = jnp.concatenate(ys, axis=0)

        z = zbuf[g]
        yz = y * (z * _sigmoid(z))
        yn = yz * lax.rsqrt(jnp.mean(yz * yz, axis=-1, keepdims=True) + NORM_EPS) * snorm_ref[:, gcols]
        yn_ref[g] = yn.astype(BF16)

    def ssd_out_partial(g_lo, g_hi):
        yn = jnp.concatenate([yn_ref[g] for g in range(g_lo, g_hi)], axis=1)
        return _dot(yn, sswo_ref[g_lo * GROUP_WIDTH:g_hi * GROUP_WIDTH, :D_MODEL])

    def project_gates(j):
        pgbuf[j] = project(wg_ref, [(part * D_MODEL + j * A_BLOCK, A_BLOCK) for part in range(2)])

    def project_ya():
        yabuf[...] = _dot(jnp.concatenate([yap_ref[j] for j in range(N_A_BLOCKS)], axis=1), swo_ref[:, :D_MODEL])

    def accumulate_yb(g_lo, g_hi):
        if g_lo == 0:
            ybbuf[...] = ssd_out_partial(g_lo, g_hi)
        else:
            ybbuf[...] += ssd_out_partial(g_lo, g_hi)

    fillers = []
    for j in range(N_A_BLOCKS):
        fillers += [functools.partial(project_a, j), functools.partial(conv_a, j)]
    fillers += [project_ya] + [functools.partial(project_gates, j) for j in range(N_A_BLOCKS)]
    for g in range(GROUPS_PROJECTED_AHEAD):
        project_xbc(g)
        project_z(g)
    dense = []
    for g in range(GROUPS_PROJECTED_AHEAD, SSM_GROUPS):
        dense.append((0, g, functools.partial(project_xbc, g)))
        dense.append((0, g, functools.partial(project_z, g)))
        if fillers:
            dense.append((0, None, fillers.pop(0)))
    dense += [(0, None, f) for f in fillers]
    for g in range(0, SSM_GROUPS, YB_GROUPS):
        dense.append((g + YB_GROUPS, None, functools.partial(accumulate_yb, g, g + YB_GROUPS)))

    live, started, finished, rounds = [], 0, 0, 0
    while started < SSM_GROUPS or live:
        if started < SSM_GROUPS and rounds % GROUP_STAGGER == 0:
            for entry in [e for e in dense if e[1] == started]:
                entry[2]()
                dense.remove(entry)
            live.append(group_stages(started))
            started += 1
        still_live = [stages for stages in live if next(stages, True) is None]
        finished += len(live) - len(still_live)
        live = still_live
        for entry in [e for e in dense if e[0] <= finished][:DENSE_PIECES_PER_ROUND[rounds % 2]]:
            entry[2]()
            dense.remove(entry)
        rounds += 1
    for entry in dense:
        entry[2]()

    for j in range(N_A_BLOCKS):
        cols = slice(j * A_BLOCK, (j + 1) * A_BLOCK)
        pg = pgbuf[j]
        merged = _sigmoid(pg[:, :A_BLOCK]) * yabuf[:, cols] + _sigmoid(pg[:, A_BLOCK:]) * ybbuf[:, cols]
        mg_ref[j] = merged.astype(BF16)
    o_ref[...] = x + _dot(jnp.concatenate([mg_ref[j] for j in range(N_A_BLOCKS)], axis=1), wo_ref[:, :D_MODEL])


def _mixer_constants(tm):
    t = np.arange(tm)
    tril = ((t[:, None] >= t[None, :]) & (t[:, None] // CHUNK == t[None, :] // CHUNK))
    k = np.arange(LANES)[:, None]
    head = np.arange(D_INNER)[None, :] // SSM_HEAD_DIM
    expand = (k % SSM_HEADS == head) & (k < 3 * SSM_HEADS)
    r = np.arange(GROUP_WIDTH)
    bdmask = (r[:, None] // SSM_HEAD_DIM) == (r[None, :] // SSM_HEAD_DIM)
    return (jnp.asarray(tril, BF16), jnp.asarray(expand, BF16), jnp.asarray(bdmask, BF16))


def _mixer(x2d, batch, seq, gain, wmain, wdt, wg, conv_a, short_w_out, conv_b, conv_b_bias, dt_bias, a_log,
           d_skip, ssm_norm, ssm_w_out, w_out, *, tm=256):
    d = D_MODEL
    ns = seq // tm
    dtb = jnp.tile(dt_bias, HEAD_COPIES).reshape(1, LANES)
    alog = jnp.tile(a_log, HEAD_COPIES).reshape(1, LANES)
    dexp = jnp.repeat(d_skip, SSM_HEAD_DIM).reshape(1, D_INNER)
    tril, expand, bdmask = _mixer_constants(tm)
    operands = [
        gain.reshape(1, d), wmain, wdt, wg,
        conv_a, short_w_out, conv_b, conv_b_bias.reshape(1, D_XBC),
        dtb, alog, dexp,
        ssm_norm.reshape(1, D_INNER), ssm_w_out, w_out,
        tril, expand, bdmask,
    ]
    x_spec = pl.BlockSpec((tm, d), lambda b, s: (b * ns + s, 0))
    scratch = [
        pltpu.VMEM((tm, d), BF16),
        pltpu.VMEM((N_A_BLOCKS, SUBLANES + tm, A_BLOCK), F32),
        pltpu.VMEM((SSM_GROUPS, SUBLANES + tm, GROUP_XBC), F32),
        pltpu.VMEM((SSM_GROUPS, tm, GROUP_WIDTH), F32),
        pltpu.VMEM((SSM_GROUPS, D_STATE, GROUP_WIDTH), F32),
        pltpu.VMEM((3, tm // LANES, LANES, LANES), F32),
        pltpu.VMEM((N_A_BLOCKS, tm, A_BLOCK), BF16),
        pltpu.VMEM((SSM_GROUPS, tm, GROUP_WIDTH), BF16),
        pltpu.VMEM((N_A_BLOCKS, tm, A_BLOCK), BF16),
        pltpu.VMEM((N_A_BLOCKS, tm, 2 * A_BLOCK), F32),
        pltpu.VMEM((tm, d), F32),
        pltpu.VMEM((tm, d), F32),
    ]
    return pl.pallas_call(
        functools.partial(_mixer_body, tm=tm),
        grid=(batch, ns),
        in_specs=[x_spec] + [_resident(op.shape) for op in operands],
        out_specs=x_spec,
        out_shape=jax.ShapeDtypeStruct(x2d.shape, F32),
        scratch_shapes=scratch,
        compiler_params=pltpu.CompilerParams(
            dimension_semantics=("arbitrary", "arbitrary"), vmem_limit_bytes=VMEM_LIMIT_BYTES),
        name="mixer",
    )(x2d, *operands)


def kernel(x, ffn1_norm, ffn1_w_in, ffn1_w_out, mix_norm, w_in, short_conv_w, short_w_out, ssm_conv_w, ssm_conv_b, ssm_dt_bias, ssm_A_log, ssm_D, ssm_norm, ssm_w_out, w_out, ffn2_norm, ffn2_w_in, ffn2_w_out, final_norm):
    b, s, d = x.shape
    x2d = x.reshape(b * s, d)
    fin = final_norm.reshape(1, d)
    w_dt_start = W_XBC_START + D_XBC
    w_g_start = w_dt_start + SSM_HEADS
    for l in range(ffn1_norm.shape[0]):
        w_t = w_in[l].T
        wdt = jnp.tile(w_t[w_dt_start:w_g_start].T, (1, HEAD_COPIES)).astype(BF16)
        jobs = (
            _CastJob(w_t, (CAST_BLOCK, d), 0, True, count=w_dt_start),
            _CastJob(w_t, (CAST_BLOCK // 2, d), 0, True, start=w_g_start),
            _CastJob(ffn2_w_in[l], (d, CAST_BLOCK // 2), 1, False),
            _CastJob(ffn2_w_out[l], (CAST_BLOCK // 2, d), 0, False),
            _CastJob(short_w_out[l], (LANES, d), 0, False),
            _CastJob(ssm_w_out[l], (LANES, d), 0, False),
            _CastJob(w_out[l], (LANES, d), 0, False),
        )
        x2d, (wmain, wg, w2_in, w2_out, swo, sswo, wo) = _ffn(
            x2d, ffn1_norm[l].reshape(1, d), ffn1_w_in[l].astype(BF16),
            jnp.pad(ffn1_w_out[l].astype(BF16), ((0, 0), (0, PITCH_PAD))), fin,
            final_norm=False, tm=512, jobs=jobs)
        x2d = _mixer(x2d, b, s, mix_norm[l], wmain, wdt, wg, short_conv_w[l], swo,
                     ssm_conv_w[l], ssm_conv_b[l], ssm_dt_bias[l], ssm_A_log[l], ssm_D[l],
                     ssm_norm[l], sswo, wo)
        last = l == ffn1_norm.shape[0] - 1
        x2d, _ = _ffn(x2d, ffn2_norm[l].reshape(1, d), w2_in, w2_out, fin, final_norm=last, tm=1024, row_parts=2)
    return x2d.reshape(b, s, d)
```

```python
import functools
from typing import NamedTuple

import jax
import jax.numpy as jnp
import numpy as np
from jax import lax
from jax.experimental import pallas as pl
from jax.experimental.pallas import tpu as pltpu

D_MODEL = 1024
D_FF = 2816
D_SHORT = D_MODEL
SHORT_CONV = 3
D_INNER = 2048
SSM_HEADS = 32
SSM_HEAD_DIM = 64
SSM_GROUPS = 8
D_STATE = 128
SSM_CONV = 4
D_XBC = D_INNER + 2 * SSM_GROUPS * D_STATE
CHUNK = 64
NORM_EPS = 1e-5

HEADS_PER_GROUP = SSM_HEADS // SSM_GROUPS
GROUP_WIDTH = HEADS_PER_GROUP * SSM_HEAD_DIM
GROUP_XBC = GROUP_WIDTH + 2 * D_STATE
LANES = 128
SUBLANES = 8
HEAD_COPIES = LANES // SSM_HEADS
A_BLOCK = 256
N_A_BLOCKS = D_SHORT // A_BLOCK
GROUP_STAGGER = 3
DENSE_PIECES_PER_ROUND = (1, 1)
GROUPS_PROJECTED_AHEAD = 2
YB_GROUPS = 2
PITCH_ALIGN = 1024
PITCH_PAD = 512
CAST_BLOCK = 512
W_Z_START = 3 * D_SHORT
W_XBC_START = W_Z_START + D_INNER

F32 = jnp.float32
BF16 = jnp.bfloat16

VMEM_LIMIT_BYTES = 58 * 1024 * 1024
MASKED = -1e30
LOG2_E = 1.4426950408889634


def _rms_norm(x, gain):
    ms = jnp.mean(x * x, axis=-1, keepdims=True)
    return x * lax.rsqrt(ms + NORM_EPS) * gain


def _sigmoid(x):
    return 1.0 / (1.0 + jnp.exp(-x))


def _dot(a, b):
    return jnp.dot(a, b, preferred_element_type=F32)


def _split3(x):
    hi = x.astype(BF16).astype(F32)
    mid = (x - hi).astype(BF16).astype(F32)
    lo = (x - hi - mid).astype(BF16).astype(F32)
    return hi, mid, lo


def _resident(shape):
    nd = len(shape)
    return pl.BlockSpec(shape, lambda *_: (0,) * nd, pipeline_mode=pl.Buffered(1))


class _CastJob(NamedTuple):
    src: jax.Array
    block: tuple
    axis: int
    transpose: bool
    start: int = 0
    count: int = 0


    @property
    def extent(self):
        return self.count or self.src.shape[self.axis] - self.start

    @property
    def n_blocks(self):
        return self.extent // self.block[self.axis]

    @property
    def dst_core_shape(self):
        rows, cols = self.src.shape
        rows, cols = (self.extent, cols) if self.axis == 0 else (rows, self.extent)
        return (cols, rows) if self.transpose else (rows, cols)

    @property
    def dst_advances_along_columns(self):
        return (self.axis == 0) == self.transpose

    @property
    def pad_blocks(self):
        if not self.dst_advances_along_columns or self.dst_core_shape[1] % PITCH_ALIGN:
            return 0
        return PITCH_PAD // self.block[self.axis]

    @property
    def pad_columns(self):
        if self.dst_advances_along_columns or self.dst_core_shape[1] % PITCH_ALIGN:
            return 0
        return PITCH_PAD

    def specs(self):
        n, axis, block = self.n_blocks, self.axis, self.block
        step = block[axis]

        def src_index(i):
            k = jnp.minimum(i, n - 1)
            if self.start:
                assert self.start % SUBLANES == 0 and step % SUBLANES == 0
                return (pl.multiple_of(self.start + k * step, SUBLANES), 0)
            return (k, 0) if axis == 0 else (0, k)

        def dst_index(i):
            k = jnp.minimum(i, n - 1 + self.pad_blocks)
            return (0, k) if self.dst_advances_along_columns else (k, 0)

        src_block = (pl.Element(block[0]), pl.Element(block[1])) if self.start else block
        dst_block = block[::-1] if self.transpose else block
        dst_block = (dst_block[0], dst_block[1] + self.pad_columns)
        rows, cols = self.dst_core_shape
        cols += self.pad_columns + self.pad_blocks * step
        return (pl.BlockSpec(src_block, src_index), pl.BlockSpec(dst_block, dst_index),
                jax.ShapeDtypeStruct((rows, cols), BF16))


def _ffn_body(x_ref, gain_ref, w_in_ref, w_out_ref, fin_ref, *rest, ff_chunk, final_norm, casts, row_parts):
    cast_src, o_ref, cast_dst = rest[:len(casts)], rest[len(casts)], rest[len(casts) + 1:]
    pending = list(zip(casts, cast_src, cast_dst))

    part_rows = x_ref.shape[0] // row_parts
    n_chunks = D_FF // ff_chunk
    for part in range(row_parts):
        rows = slice(part * part_rows, (part + 1) * part_rows)
        x = x_ref[rows, :]
        h = _rms_norm(x, gain_ref[...]).astype(BF16)
        acc = jnp.zeros(x.shape, F32)
        for j in range(n_chunks):
            lo = j * ff_chunk
            g = _dot(h, w_in_ref[:, lo:lo + ff_chunk])
            u = _dot(h, w_in_ref[:, D_FF + lo:D_FF + lo + ff_chunk])
            a = (g * _sigmoid(g) * u).astype(BF16)
            acc = acc + _dot(a, w_out_ref[lo:lo + ff_chunk, :D_MODEL])
            if part == 0:
                for (transpose, pad), src_ref, dst_ref in (pending[j:j + 1] if j < n_chunks - 1 else pending[j:]):
                    v = src_ref[...]
                    v = v.astype(BF16).T if transpose else v.astype(BF16)
                    if pad:
                        dst_ref[:, :v.shape[1]] = v
                        dst_ref[:, v.shape[1]:] = jnp.zeros((v.shape[0], pad), BF16)
                    else:
                        dst_ref[...] = v
        y = x + 0.5 * acc
        if final_norm:
            y = _rms_norm(y, fin_ref[...])
        o_ref[rows, :] = y


def _ffn(x2d, gain, w_in, w_out, fin, *, final_norm, tm, ff_chunk=256, jobs=(), row_parts=1):
    t, d = x2d.shape
    n_steps = t // tm
    assert all(job.n_blocks <= n_steps for job in jobs)
    job_specs = [job.specs() for job in jobs]
    body = functools.partial(_ffn_body, ff_chunk=ff_chunk, final_norm=final_norm, row_parts=row_parts,
                             casts=tuple((job.transpose, job.pad_columns) for job in jobs))
    x_spec = pl.BlockSpec((tm, d), lambda i: (i, 0))
    outs = pl.pallas_call(
        body,
        grid=(n_steps,),
        in_specs=[x_spec, _resident((1, d)), _resident(w_in.shape), _resident(w_out.shape), _resident((1, d))]
        + [s[0] for s in job_specs],
        out_specs=[x_spec] + [s[1] for s in job_specs],
        out_shape=[jax.ShapeDtypeStruct((t, d), F32)] + [s[2] for s in job_specs],
        compiler_params=pltpu.CompilerParams(
            dimension_semantics=("arbitrary",), vmem_limit_bytes=VMEM_LIMIT_BYTES),
        name="ffn_final" if final_norm else "ffn",
    )(x2d, gain, w_in, w_out, fin, *[job.src for job in jobs])
    return outs[0], outs[1:]


def _mixer_body(x_ref, gain_ref, wmain_ref, wdt_ref, wg_ref,
                cwa_ref, swo_ref, cwb_ref, cbb_ref, dtb_ref, alog_ref, dexp_ref,
                snorm_ref, sswo_ref, wo_ref, tril_ref, expand_ref, bdmask_ref,
                o_ref,
                hb_ref, ubuf, xbuf, zbuf, st_ref, tr_ref, yap_ref, yn_ref, mg_ref, pgbuf, yabuf, ybbuf,
                *, tm):
    n_chunks = tm // CHUNK

    @pl.when(pl.program_id(1) == 0)
    def _():
        ubuf[:, 0:SUBLANES, :] = jnp.zeros((N_A_BLOCKS, SUBLANES, A_BLOCK), F32)
        xbuf[:, 0:SUBLANES, :] = jnp.zeros((SSM_GROUPS, SUBLANES, GROUP_XBC), F32)
        st_ref[...] = jnp.zeros(st_ref.shape, F32)

    x = x_ref[...]
    hb_ref[...] = _rms_norm(x, gain_ref[...]).astype(BF16)

    def project(w_ref, pieces):
        w = [w_ref[:, lo:lo + width] for lo, width in pieces]
        return _dot(hb_ref[...], w[0] if len(w) == 1 else jnp.concatenate(w, axis=1))

    def xbc_pieces(g):
        return [(g * GROUP_WIDTH, GROUP_WIDTH),
                (D_INNER + g * D_STATE, D_STATE),
                (D_INNER + (SSM_GROUPS + g) * D_STATE, D_STATE)]

    def gather_xbc(ref, rows, g):
        return jnp.concatenate([ref[rows, lo:lo + width] for lo, width in xbc_pieces(g)], axis=1)

    pre = _dot(hb_ref[...], wdt_ref[...]) + dtb_ref[...]
    dt = jnp.maximum(pre, 0.0) + jnp.log1p(jnp.exp(-jnp.abs(pre)))
    a = dt * (-jnp.exp(alog_ref[...]))
    a_hi, a_mid, a_lo = (part.astype(BF16) for part in _split3(a))
    tril = tril_ref[...]
    cs = (_dot(tril, a_hi) + _dot(tril, a_mid) + _dot(tril, a_lo)) * LOG2_E
    cs_last = jnp.concatenate(
        [jnp.broadcast_to(cs[(c + 1) * CHUNK - 1:(c + 1) * CHUNK, :], (CHUNK, LANES)) for c in range(n_chunks)], axis=0)
    to_end = dt * jnp.exp2(cs_last - cs)
    cs_hi, cs_mid, cs_lo = _split3(cs)
    lane_t = lax.broadcasted_iota(jnp.int32, (tm, LANES), 1)
    packed = jnp.where(lane_t < SSM_HEADS, cs_hi,
                       jnp.where(lane_t < 2 * SSM_HEADS, cs_mid,
                                 jnp.where(lane_t < 3 * SSM_HEADS, cs_lo, 0.0))).astype(BF16)

    lane_8 = lax.broadcasted_iota(jnp.int32, (SSM_GROUPS, LANES), 1)

    def along_lanes(v, slot):
        v_t = v.T
        for blk in range(tm // LANES):
            tr_ref[slot, blk] = v_t[:, blk * LANES:(blk + 1) * LANES]
        out = []
        for c in range(n_chunks):
            blk = c // 2
            pieces = [tr_ref[slot, blk, pl.ds(hh, SSM_GROUPS, stride=HEADS_PER_GROUP), :]
                      for hh in range(HEADS_PER_GROUP)]
            rolled = [pltpu.roll(p, CHUNK, axis=1) for p in pieces]
            if c % 2 == 0:
                halves = [jnp.where(lane_8 < CHUNK, pieces[0], rolled[1]),
                          jnp.where(lane_8 < CHUNK, pieces[2], rolled[3])]
            else:
                halves = [jnp.where(lane_8 < CHUNK, rolled[0], pieces[1]),
                          jnp.where(lane_8 < CHUNK, rolled[2], pieces[3])]
            out.append(jnp.concatenate(halves, axis=1))
        return out

    cs_rows = along_lanes(cs, 0)
    dt_rows = along_lanes(dt, 1)
    to_end_rows = along_lanes(to_end, 2)

    a_proj = {}

    def project_a(j):
        a_proj[j] = project(wmain_ref, [(part * D_SHORT + j * A_BLOCK, A_BLOCK) for part in range(3)])

    def project_xbc(g):
        xbuf[g, SUBLANES:SUBLANES + tm, :] = project(
            wmain_ref, [(W_XBC_START + lo, width) for lo, width in xbc_pieces(g)])

    def project_z(g):
        zbuf[g] = project(wmain_ref, [(W_Z_START + g * GROUP_WIDTH, GROUP_WIDTH)])

    def conv_a(j):
        cols = slice(j * A_BLOCK, (j + 1) * A_BLOCK)
        p = a_proj.pop(j)
        u = p[:, A_BLOCK:2 * A_BLOCK] * p[:, 2 * A_BLOCK:]
        ubuf[j, SUBLANES:SUBLANES + tm, :] = u
        va = cwa_ref[SHORT_CONV - 1:SHORT_CONV, cols] * u
        for k in range(SHORT_CONV - 1):
            off = SUBLANES - (SHORT_CONV - 1) + k
            va = va + cwa_ref[k:k + 1, cols] * ubuf[j, off:off + tm, :]
        ubuf[j, 0:SUBLANES, :] = ubuf[j, tm:tm + SUBLANES, :]
        yap_ref[j] = (p[:, :A_BLOCK] * va).astype(BF16)

    lane_sq = lax.broadcasted_iota(jnp.int32, (LANES, LANES), 1)
    row_i = lax.broadcasted_iota(jnp.int32, (CHUNK, GROUP_WIDTH), 0)
    lane_j = lax.broadcasted_iota(jnp.int32, (CHUNK, GROUP_WIDTH), 1) % CHUNK
    causal = row_i >= lane_j
    bdmask = bdmask_ref[...]

    def group_stages(g):
        gcols = slice(g * GROUP_WIDTH, (g + 1) * GROUP_WIDTH)
        acc = (gather_xbc(cbb_ref, slice(None), g)
               + gather_xbc(cwb_ref, slice(SSM_CONV - 1, SSM_CONV), g) * xbuf[g, SUBLANES:SUBLANES + tm, :])
        for k in range(SSM_CONV - 1):
            off = SUBLANES - (SSM_CONV - 1) + k
            acc = acc + gather_xbc(cwb_ref, slice(k, k + 1), g) * xbuf[g, off:off + tm, :]
        xbuf[g, 0:SUBLANES, :] = xbuf[g, tm:tm + SUBLANES, :]
        v = acc * _sigmoid(acc)
        yield
        xs = v[:, :GROUP_WIDTH]
        xs_bf = xs.astype(BF16)
        cmg = v[:, GROUP_WIDTH + D_STATE:].astype(BF16)
        bm_t = v[:, GROUP_WIDTH:GROUP_WIDTH + D_STATE].T
        bt = []
        for blk in range(tm // LANES):
            w = bm_t[:, blk * LANES:(blk + 1) * LANES]
            r = pltpu.roll(w, CHUNK, axis=1)
            bt.append(jnp.where(lane_sq < CHUNK, w, r))
            bt.append(jnp.where(lane_sq < CHUNK, r, w))
        cscol = _dot(packed, expand_ref[:, gcols])
        yield

        state = st_ref[g]
        ys = []
        for c in range(n_chunks):
            rows = slice(c * CHUNK, (c + 1) * CHUNK)
            col = cscol[rows]
            cm_c = cmg[rows]
            bt4 = jnp.concatenate([bt[c], bt[c]], axis=1)
            cb4 = _dot(cm_c, bt4.astype(BF16))
            decay = jnp.exp2(jnp.where(causal, col - cs_rows[c][g:g + 1, :], MASKED))
            m = (cb4 * decay * dt_rows[c][g:g + 1, :]).astype(BF16)
            b_end = (bt4 * to_end_rows[c][g:g + 1, :]).astype(BF16)
            yield
            bd = jnp.concatenate([xs_bf[rows]] * HEADS_PER_GROUP, axis=0) * bdmask
            both = _dot(jnp.concatenate([m, b_end], axis=0), bd)
            y_off = _dot(cm_c, state.astype(BF16)) * jnp.exp2(col)
            ys.append(both[:CHUNK] + y_off + xs[rows] * dexp_ref[:, gcols])
            state = state * jnp.exp2(col[CHUNK - 1:CHUNK, :]) + both[CHUNK:]
            yield
        st_ref[g] = state
        y = jnp.concatenate(ys, axis=0)

        z = zbuf[g]
        yz = y * (z * _sigmoid(z))
        yn = yz * lax.rsqrt(jnp.mean(yz * yz, axis=-1, keepdims=True) + NORM_EPS) * snorm_ref[:, gcols]
        yn_ref[g] = yn.astype(BF16)

    def ssd_out_partial(g_lo, g_hi):
        yn = jnp.concatenate([yn_ref[g] for g in range(g_lo, g_hi)], axis=1)
        return _dot(yn, sswo_ref[g_lo * GROUP_WIDTH:g_hi * GROUP_WIDTH, :D_MODEL])

    def project_gates(j):
        pgbuf[j] = project(wg_ref, [(part * D_MODEL + j * A_BLOCK, A_BLOCK) for part in range(2)])

    def project_ya():
        yabuf[...] = _dot(jnp.concatenate([yap_ref[j] for j in range(N_A_BLOCKS)], axis=1), swo_ref[:, :D_MODEL])

    def accumulate_yb(g_lo, g_hi):
        if g_lo == 0:
            ybbuf[...] = ssd_out_partial(g_lo, g_hi)
        else:
            ybbuf[...] += ssd_out_partial(g_lo, g_hi)

    fillers = []
    for j in range(N_A_BLOCKS):
        fillers += [functools.partial(project_a, j), functools.partial(conv_a, j)]
    fillers += [project_ya] + [functools.partial(project_gates, j) for j in range(N_A_BLOCKS)]
    for g in range(GROUPS_PROJECTED_AHEAD):
        project_xbc(g)
        project_z(g)
    dense = []
    for g in range(GROUPS_PROJECTED_AHEAD, SSM_GROUPS):
        dense.append((0, g, functools.partial(project_xbc, g)))
        dense.append((0, g, functools.partial(project_z, g)))
        if fillers:
            dense.append((0, None, fillers.pop(0)))
    dense += [(0, None, f) for f in fillers]
    for g in range(0, SSM_GROUPS, YB_GROUPS):
        dense.append((g + YB_GROUPS, None, functools.partial(accumulate_yb, g, g + YB_GROUPS)))

    live, started, finished, rounds = [], 0, 0, 0
    while started < SSM_GROUPS or live:
        if started < SSM_GROUPS and rounds % GROUP_STAGGER == 0:
            for entry in [e for e in dense if e[1] == started]:
                entry[2]()
                dense.remove(entry)
            live.append(group_stages(started))
            started += 1
        still_live = [stages for stages in live if next(stages, True) is None]
        finished += len(live) - len(still_live)
        live = still_live
        for entry in [e for e in dense if e[0] <= finished][:DENSE_PIECES_PER_ROUND[rounds % 2]]:
            entry[2]()
            dense.remove(entry)
        rounds += 1
    for entry in dense:
        entry[2]()

    for j in range(N_A_BLOCKS):
        cols = slice(j * A_BLOCK, (j + 1) * A_BLOCK)
        pg = pgbuf[j]
        merged = _sigmoid(pg[:, :A_BLOCK]) * yabuf[:, cols] + _sigmoid(pg[:, A_BLOCK:]) * ybbuf[:, cols]
        mg_ref[j] = merged.astype(BF16)
    o_ref[...] = x + _dot(jnp.concatenate([mg_ref[j] for j in range(N_A_BLOCKS)], axis=1), wo_ref[:, :D_MODEL])


def _mixer_constants(tm):
    t = np.arange(tm)
    tril = ((t[:, None] >= t[None, :]) & (t[:, None] // CHUNK == t[None, :] // CHUNK))
    k = np.arange(LANES)[:, None]
    head = np.arange(D_INNER)[None, :] // SSM_HEAD_DIM
    expand = (k % SSM_HEADS == head) & (k < 3 * SSM_HEADS)
    r = np.arange(GROUP_WIDTH)
    bdmask = (r[:, None] // SSM_HEAD_DIM) == (r[None, :] // SSM_HEAD_DIM)
    return (jnp.asarray(tril, BF16), jnp.asarray(expand, BF16), jnp.asarray(bdmask, BF16))


def _mixer(x2d, batch, seq, gain, wmain, wdt, wg, conv_a, short_w_out, conv_b, conv_b_bias, dt_bias, a_log,
           d_skip, ssm_norm, ssm_w_out, w_out, *, tm=256):
    d = D_MODEL
    ns = seq // tm
    dtb = jnp.tile(dt_bias, HEAD_COPIES).reshape(1, LANES)
    alog = jnp.tile(a_log, HEAD_COPIES).reshape(1, LANES)
    dexp = jnp.repeat(d_skip, SSM_HEAD_DIM).reshape(1, D_INNER)
    tril, expand, bdmask = _mixer_constants(tm)
    operands = [
        gain.reshape(1, d), wmain, wdt, wg,
        conv_a, short_w_out, conv_b, conv_b_bias.reshape(1, D_XBC),
        dtb, alog, dexp,
        ssm_norm.reshape(1, D_INNER), ssm_w_out, w_out,
        tril, expand, bdmask,
    ]
    x_spec = pl.BlockSpec((tm, d), lambda b, s: (b * ns + s, 0))
    scratch = [
        pltpu.VMEM((tm, d), BF16),
        pltpu.VMEM((N_A_BLOCKS, SUBLANES + tm, A_BLOCK), F32),
        pltpu.VMEM((SSM_GROUPS, SUBLANES + tm, GROUP_XBC), F32),
        pltpu.VMEM((SSM_GROUPS, tm, GROUP_WIDTH), F32),
        pltpu.VMEM((SSM_GROUPS, D_STATE, GROUP_WIDTH), F32),
        pltpu.VMEM((3, tm // LANES, LANES, LANES), F32),
        pltpu.VMEM((N_A_BLOCKS, tm, A_BLOCK), BF16),
        pltpu.VMEM((SSM_GROUPS, tm, GROUP_WIDTH), BF16),
        pltpu.VMEM((N_A_BLOCKS, tm, A_BLOCK), BF16),
        pltpu.VMEM((N_A_BLOCKS, tm, 2 * A_BLOCK), F32),
        pltpu.VMEM((tm, d), F32),
        pltpu.VMEM((tm, d), F32),
    ]
    return pl.pallas_call(
        functools.partial(_mixer_body, tm=tm),
        grid=(batch, ns),
        in_specs=[x_spec] + [_resident(op.shape) for op in operands],
        out_specs=x_spec,
        out_shape=jax.ShapeDtypeStruct(x2d.shape, F32),
        scratch_shapes=scratch,
        compiler_params=pltpu.CompilerParams(
            dimension_semantics=("arbitrary", "arbitrary"), vmem_limit_bytes=VMEM_LIMIT_BYTES),
        name="mixer",
    )(x2d, *operands)


def kernel(x, ffn1_norm, ffn1_w_in, ffn1_w_out, mix_norm, w_in, short_conv_w, short_w_out, ssm_conv_w, ssm_conv_b, ssm_dt_bias, ssm_A_log, ssm_D, ssm_norm, ssm_w_out, w_out, ffn2_norm, ffn2_w_in, ffn2_w_out, final_norm):
    b, s, d = x.shape
    x2d = x.reshape(b * s, d)
    fin = final_norm.reshape(1, d)
    w_dt_start = W_XBC_START + D_XBC
    w_g_start = w_dt_start + SSM_HEADS
    for l in range(ffn1_norm.shape[0]):
        w_t = w_in[l].T
        wdt = jnp.tile(w_t[w_dt_start:w_g_start].T, (1, HEAD_COPIES)).astype(BF16)
        jobs = (
            _CastJob(w_t, (CAST_BLOCK, d), 0, True, count=w_dt_start),
            _CastJob(w_t, (CAST_BLOCK // 2, d), 0, True, start=w_g_start),
            _CastJob(ffn2_w_in[l], (d, CAST_BLOCK // 2), 1, False),
            _CastJob(ffn2_w_out[l], (CAST_BLOCK // 2, d), 0, False),
            _CastJob(short_w_out[l], (LANES, d), 0, False),
            _CastJob(ssm_w_out[l], (LANES, d), 0, False),
            _CastJob(w_out[l], (LANES, d), 0, False),
        )
        x2d, (wmain, wg, w2_in, w2_out, swo, sswo, wo) = _ffn(
            x2d, ffn1_norm[l].reshape(1, d), ffn1_w_in[l].astype(BF16),
            jnp.pad(ffn1_w_out[l].astype(BF16), ((0, 0), (0, PITCH_PAD))), fin,
            final_norm=False, tm=512, jobs=jobs)
        x2d = _mixer(x2d, b, s, mix_norm[l], wmain, wdt, wg, short_conv_w[l], swo,
                     ssm_conv_w[l], ssm_conv_b[l], ssm_dt_bias[l], ssm_A_log[l], ssm_D[l],
                     ssm_norm[l], sswo, wo)
        last = l == ffn1_norm.shape[0] - 1
        x2d, _ = _ffn(x2d, ffn2_norm[l].reshape(1, d), w2_in, w2_out, fin, final_norm=last, tm=1024, row_parts=2)
    return x2d.reshape(b, s, d)
```

```python
import functools
from typing import NamedTuple

import jax
import jax.numpy as jnp
import numpy as np
from jax import lax
from jax.experimental import pallas as pl
from jax.experimental.pallas import tpu as pltpu

D_MODEL = 1024
D_FF = 2816
D_SHORT = D_MODEL
SHORT_CONV = 3
D_INNER = 2048
SSM_HEADS = 32
SSM_HEAD_DIM = 64
SSM_GROUPS = 8
D_STATE = 128
SSM_CONV = 4
D_XBC = D_INNER + 2 * SSM_GROUPS * D_STATE
CHUNK = 64
NORM_EPS = 1e-5

HEADS_PER_GROUP = SSM_HEADS // SSM_GROUPS
GROUP_WIDTH = HEADS_PER_GROUP * SSM_HEAD_DIM
GROUP_XBC = GROUP_WIDTH + 2 * D_STATE
LANES = 128
SUBLANES = 8
HEAD_COPIES = LANES // SSM_HEADS
A_BLOCK = 256
N_A_BLOCKS = D_SHORT // A_BLOCK
GROUP_STAGGER = 3
DENSE_PIECES_PER_ROUND = (1, 1)
GROUPS_PROJECTED_AHEAD = 2
YB_GROUPS = 2
PITCH_ALIGN = 1024
PITCH_PAD = 512
CAST_BLOCK = 512
W_Z_START = 3 * D_SHORT
W_XBC_START = W_Z_START + D_INNER

F32 = jnp.float32
BF16 = jnp.bfloat16

VMEM_LIMIT_BYTES = 58 * 1024 * 1024
MASKED = -1e30
LOG2_E = 1.4426950408889634


def _rms_norm(x, gain):
    ms = jnp.mean(x * x, axis=-1, keepdims=True)
    return x * lax.rsqrt(ms + NORM_EPS) * gain


def _sigmoid(x):
    return 1.0 / (1.0 + jnp.exp(-x))


def _dot(a, b):
    return jnp.dot(a, b, preferred_element_type=F32)


def _split3(x):
    hi = x.astype(BF16).astype(F32)
    mid = (x - hi).astype(BF16).astype(F32)
    lo = (x - hi - mid).astype(BF16).astype(F32)
    return hi, mid, lo


def _resident(shape):
    nd = len(shape)
    return pl.BlockSpec(shape, lambda *_: (0,) * nd, pipeline_mode=pl.Buffered(1))


class _CastJob(NamedTuple):
    src: jax.Array
    block: tuple
    axis: int
    transpose: bool
    start: int = 0
    count: int = 0


    @property
    def extent(self):
        return self.count or self.src.shape[self.axis] - self.start

    @property
    def n_blocks(self):
        return self.extent // self.block[self.axis]

    @property
    def dst_core_shape(self):
        rows, cols = self.src.shape
        rows, cols = (self.extent, cols) if self.axis == 0 else (rows, self.extent)
        return (cols, rows) if self.transpose else (rows, cols)

    @property
    def dst_advances_along_columns(self):
        return (self.axis == 0) == self.transpose

    @property
    def pad_blocks(self):
        if not self.dst_advances_along_columns or self.dst_core_shape[1] % PITCH_ALIGN:
            return 0
        return PITCH_PAD // self.block[self.axis]

    @property
    def pad_columns(self):
        if self.dst_advances_along_columns or self.dst_core_shape[1] % PITCH_ALIGN:
            return 0
        return PITCH_PAD

    def specs(self):
        n, axis, block = self.n_blocks, self.axis, self.block
        step = block[axis]

        def src_index(i):
            k = jnp.minimum(i, n - 1)
            if self.start:
                assert self.start % SUBLANES == 0 and step % SUBLANES == 0
                return (pl.multiple_of(self.start + k * step, SUBLANES), 0)
            return (k, 0) if axis == 0 else (0, k)

        def dst_index(i):
            k = jnp.minimum(i, n - 1 + self.pad_blocks)
            return (0, k) if self.dst_advances_along_columns else (k, 0)

        src_block = (pl.Element(block[0]), pl.Element(block[1])) if self.start else block
        dst_block = block[::-1] if self.transpose else block
        dst_block = (dst_block[0], dst_block[1] + self.pad_columns)
        rows, cols = self.dst_core_shape
        cols += self.pad_columns + self.pad_blocks * step
        return (pl.BlockSpec(src_block, src_index), pl.BlockSpec(dst_block, dst_index),
                jax.ShapeDtypeStruct((rows, cols), BF16))


def _ffn_body(x_ref, gain_ref, w_in_ref, w_out_ref, fin_ref, *rest, ff_chunk, final_norm, casts, row_parts):
    cast_src, o_ref, cast_dst = rest[:len(casts)], rest[len(casts)], rest[len(casts) + 1:]
    pending = list(zip(casts, cast_src, cast_dst))

    part_rows = x_ref.shape[0] // row_parts
    n_chunks = D_FF // ff_chunk
    for part in range(row_parts):
        rows = slice(part * part_rows, (part + 1) * part_rows)
        x = x_ref[rows, :]
        h = _rms_norm(x, gain_ref[...]).astype(BF16)
        acc = jnp.zeros(x.shape, F32)

        def gate_up(j):
            lo = j * ff_chunk
            return (_dot(h, w_in_ref[:, lo:lo + ff_chunk]),
                    _dot(h, w_in_ref[:, D_FF + lo:D_FF + lo + ff_chunk]))

        ahead = gate_up(0)
        for j in range(n_chunks):
            lo = j * ff_chunk
            g, u = ahead
            if j + 1 < n_chunks:
                ahead = gate_up(j + 1)
            a = (g * _sigmoid(g) * u).astype(BF16)
            acc = acc + _dot(a, w_out_ref[lo:lo + ff_chunk, :D_MODEL])
            if part == 0:
                for (transpose, pad), src_ref, dst_ref in (pending[j:j + 1] if j < n_chunks - 1 else pending[j:]):
                    v = src_ref[...]
                    v = v.astype(BF16).T if transpose else v.astype(BF16)
                    if pad:
                        dst_ref[:, :v.shape[1]] = v
                        dst_ref[:, v.shape[1]:] = jnp.zeros((v.shape[0], pad), BF16)
                    else:
                        dst_ref[...] = v
        y = x + 0.5 * acc
        if final_norm:
            y = _rms_norm(y, fin_ref[...])
        o_ref[rows, :] = y


def _ffn(x2d, gain, w_in, w_out, fin, *, final_norm, tm, ff_chunk=256, jobs=(), row_parts=1):
    t, d = x2d.shape
    n_steps = t // tm
    assert all(job.n_blocks <= n_steps for job in jobs)
    job_specs = [job.specs() for job in jobs]
    body = functools.partial(_ffn_body, ff_chunk=ff_chunk, final_norm=final_norm, row_parts=row_parts,
                             casts=tuple((job.transpose, job.pad_columns) for job in jobs))
    x_spec = pl.BlockSpec((tm, d), lambda i: (i, 0))
    outs = pl.pallas_call(
        body,
        grid=(n_steps,),
        in_specs=[x_spec, _resident((1, d)), _resident(w_in.shape), _resident(w_out.shape), _resident((1, d))]
        + [s[0] for s in job_specs],
        out_specs=[x_spec] + [s[1] for s in job_specs],
        out_shape=[jax.ShapeDtypeStruct((t, d), F32)] + [s[2] for s in job_specs],
        compiler_params=pltpu.CompilerParams(
            dimension_semantics=("arbitrary",), vmem_limit_bytes=VMEM_LIMIT_BYTES),
        name="ffn_final" if final_norm else "ffn",
    )(x2d, gain, w_in, w_out, fin, *[job.src for job in jobs])
    return outs[0], outs[1:]


def _mixer_body(x_ref, gain_ref, wmain_ref, wdt_ref, wg_ref,
                cwa_ref, swo_ref, cwb_ref, cbb_ref, dtb_ref, alog_ref, dexp_ref,
                snorm_ref, sswo_ref, wo_ref, tril_ref, expand_ref, bdmask_ref,
                o_ref,
                hb_ref, ubuf, xbuf, zbuf, st_ref, tr_ref, yap_ref, yn_ref, mg_ref, pgbuf, yabuf, ybbuf,
                *, tm):
    n_chunks = tm // CHUNK

    @pl.when(pl.program_id(1) == 0)
    def _():
        ubuf[:, 0:SUBLANES, :] = jnp.zeros((N_A_BLOCKS, SUBLANES, A_BLOCK), F32)
        xbuf[:, 0:SUBLANES, :] = jnp.zeros((SSM_GROUPS, SUBLANES, GROUP_XBC), F32)
        st_ref[...] = jnp.zeros(st_ref.shape, F32)

    hb_ref[...] = _rms_norm(x_ref[...], gain_ref[...]).astype(BF16)

    def project(w_ref, pieces):
        w = [w_ref[:, lo:lo + width] for lo, width in pieces]
        return _dot(hb_ref[...], w[0] if len(w) == 1 else jnp.concatenate(w, axis=1))

    def xbc_pieces(g):
        return [(g * GROUP_WIDTH, GROUP_WIDTH),
                (D_INNER + g * D_STATE, D_STATE),
                (D_INNER + (SSM_GROUPS + g) * D_STATE, D_STATE)]

    def gather_xbc(ref, rows, g):
        return jnp.concatenate([ref[rows, lo:lo + width] for lo, width in xbc_pieces(g)], axis=1)

    pre = _dot(hb_ref[...], wdt_ref[...]) + dtb_ref[...]
    dt = jnp.maximum(pre, 0.0) + jnp.log1p(jnp.exp(-jnp.abs(pre)))
    a = dt * (-jnp.exp(alog_ref[...]))
    a_hi, a_mid, a_lo = (part.astype(BF16) for part in _split3(a))
    tril = tril_ref[...]
    cs = (_dot(tril, a_hi) + _dot(tril, a_mid) + _dot(tril, a_lo)) * LOG2_E
    cs_last = jnp.concatenate(
        [jnp.broadcast_to(cs[(c + 1) * CHUNK - 1:(c + 1) * CHUNK, :], (CHUNK, LANES)) for c in range(n_chunks)], axis=0)
    to_end = dt * jnp.exp2(cs_last - cs)
    cs_hi, cs_mid, cs_lo = _split3(cs)
    lane_t = lax.broadcasted_iota(jnp.int32, (tm, LANES), 1)
    packed = jnp.where(lane_t < SSM_HEADS, cs_hi,
                       jnp.where(lane_t < 2 * SSM_HEADS, cs_mid,
                                 jnp.where(lane_t < 3 * SSM_HEADS, cs_lo, 0.0))).astype(BF16)

    lane_8 = lax.broadcasted_iota(jnp.int32, (SSM_GROUPS, LANES), 1)

    def along_lanes(v, slot):
        v_t = v.T
        for blk in range(tm // LANES):
            tr_ref[slot, blk] = v_t[:, blk * LANES:(blk + 1) * LANES]
        out = []
        for c in range(n_chunks):
            blk = c // 2
            pieces = [tr_ref[slot, blk, pl.ds(hh, SSM_GROUPS, stride=HEADS_PER_GROUP), :]
                      for hh in range(HEADS_PER_GROUP)]
            rolled = [pltpu.roll(p, CHUNK, axis=1) for p in pieces]
            if c % 2 == 0:
                halves = [jnp.where(lane_8 < CHUNK, pieces[0], rolled[1]),
                          jnp.where(lane_8 < CHUNK, pieces[2], rolled[3])]
            else:
                halves = [jnp.where(lane_8 < CHUNK, rolled[0], pieces[1]),
                          jnp.where(lane_8 < CHUNK, rolled[2], pieces[3])]
            out.append(jnp.concatenate(halves, axis=1))
        return out

    cs_rows = along_lanes(cs, 0)
    dt_rows = along_lanes(dt, 1)
    to_end_rows = along_lanes(to_end, 2)

    a_proj = {}

    def project_a(j):
        a_proj[j] = project(wmain_ref, [(part * D_SHORT + j * A_BLOCK, A_BLOCK) for part in range(3)])

    def project_xbc(g):
        xbuf[g, SUBLANES:SUBLANES + tm, :] = project(
            wmain_ref, [(W_XBC_START + lo, width) for lo, width in xbc_pieces(g)])

    def project_z(g):
        zbuf[g] = project(wmain_ref, [(W_Z_START + g * GROUP_WIDTH, GROUP_WIDTH)])

    def conv_a(j):
        cols = slice(j * A_BLOCK, (j + 1) * A_BLOCK)
        p = a_proj.pop(j)
        u = p[:, A_BLOCK:2 * A_BLOCK] * p[:, 2 * A_BLOCK:]
        ubuf[j, SUBLANES:SUBLANES + tm, :] = u
        va = cwa_ref[SHORT_CONV - 1:SHORT_CONV, cols] * u
        for k in range(SHORT_CONV - 1):
            off = SUBLANES - (SHORT_CONV - 1) + k
            va = va + cwa_ref[k:k + 1, cols] * ubuf[j, off:off + tm, :]
        ubuf[j, 0:SUBLANES, :] = ubuf[j, tm:tm + SUBLANES, :]
        yap_ref[j] = (p[:, :A_BLOCK] * va).astype(BF16)

    lane_sq = lax.broadcasted_iota(jnp.int32, (LANES, LANES), 1)
    row_i = lax.broadcasted_iota(jnp.int32, (CHUNK, GROUP_WIDTH), 0)
    lane_j = lax.broadcasted_iota(jnp.int32, (CHUNK, GROUP_WIDTH), 1) % CHUNK
    causal = row_i >= lane_j
    bdmask = bdmask_ref[...]

    def group_stages(g):
        gcols = slice(g * GROUP_WIDTH, (g + 1) * GROUP_WIDTH)
        acc = (gather_xbc(cbb_ref, slice(None), g)
               + gather_xbc(cwb_ref, slice(SSM_CONV - 1, SSM_CONV), g) * xbuf[g, SUBLANES:SUBLANES + tm, :])
        for k in range(SSM_CONV - 1):
            off = SUBLANES - (SSM_CONV - 1) + k
            acc = acc + gather_xbc(cwb_ref, slice(k, k + 1), g) * xbuf[g, off:off + tm, :]
        xbuf[g, 0:SUBLANES, :] = xbuf[g, tm:tm + SUBLANES, :]
        v = acc * _sigmoid(acc)
        yield
        xs = v[:, :GROUP_WIDTH]
        xs_bf = xs.astype(BF16)
        cmg = v[:, GROUP_WIDTH + D_STATE:].astype(BF16)
        bm_t = v[:, GROUP_WIDTH:GROUP_WIDTH + D_STATE].T
        bt = []
        for blk in range(tm // LANES):
            w = bm_t[:, blk * LANES:(blk + 1) * LANES]
            r = pltpu.roll(w, CHUNK, axis=1)
            bt.append(jnp.where(lane_sq < CHUNK, w, r))
            bt.append(jnp.where(lane_sq < CHUNK, r, w))
        cs_view = xbuf.at[g, SUBLANES:SUBLANES + tm, GROUP_WIDTH:]
        y_view = xbuf.at[g, SUBLANES:SUBLANES + tm, :GROUP_WIDTH]
        cs_view[...] = _dot(packed, expand_ref[:, gcols])
        yield

        state = st_ref[g]

        def first_phase(c, state):
            rows = slice(c * CHUNK, (c + 1) * CHUNK)
            col = cs_view[rows, :]
            bt4 = jnp.concatenate([bt[c], bt[c]], axis=1)
            cb4 = _dot(cmg[rows], bt4.astype(BF16))
            y_off = _dot(cmg[rows], state.astype(BF16))
            decay_dt = jnp.exp2(jnp.where(causal, col - cs_rows[c][g:g + 1, :], MASKED)) * dt_rows[c][g:g + 1, :]
            b_end = (bt4 * to_end_rows[c][g:g + 1, :]).astype(BF16)
            bd = jnp.concatenate([xs_bf[rows]] * HEADS_PER_GROUP, axis=0) * bdmask
            return cb4, y_off, decay_dt, b_end, bd

        pending = first_phase(0, state)
        yield
        for c in range(n_chunks):
            rows = slice(c * CHUNK, (c + 1) * CHUNK)
            col = cs_view[rows, :]
            cb4, y_off, decay_dt, b_end, bd = pending
            m = (cb4 * decay_dt).astype(BF16)
            both = _dot(jnp.concatenate([b_end, m], axis=0), bd)
            yield
            y_view[rows, :] = both[D_STATE:] + y_off * jnp.exp2(col) + xs[rows] * dexp_ref[:, gcols]
            state = state * jnp.exp2(col[CHUNK - 1:CHUNK, :]) + both[:D_STATE]
            if c + 1 < n_chunks:
                pending = first_phase(c + 1, state)
                yield
        st_ref[g] = state
        y = y_view[...]

        z = zbuf[g]
        yz = y * (z * _sigmoid(z))
        yn = yz * lax.rsqrt(jnp.mean(yz * yz, axis=-1, keepdims=True) + NORM_EPS) * snorm_ref[:, gcols]
        yn_ref[g] = yn.astype(BF16)

    def ssd_out_partial(g_lo, g_hi):
        yn = jnp.concatenate([yn_ref[g] for g in range(g_lo, g_hi)], axis=1)
        return _dot(yn, sswo_ref[g_lo * GROUP_WIDTH:g_hi * GROUP_WIDTH, :D_MODEL])

    def project_gates(j):
        pgbuf[j] = project(wg_ref, [(part * D_MODEL + j * A_BLOCK, A_BLOCK) for part in range(2)])

    def project_ya():
        yabuf[...] = _dot(jnp.concatenate([yap_ref[j] for j in range(N_A_BLOCKS)], axis=1), swo_ref[:, :D_MODEL])

    def accumulate_yb(g_lo, g_hi):
        if g_lo == 0:
            ybbuf[...] = ssd_out_partial(g_lo, g_hi)
        else:
            ybbuf[...] += ssd_out_partial(g_lo, g_hi)

    fillers = []
    for j in range(N_A_BLOCKS):
        fillers += [functools.partial(project_a, j), functools.partial(conv_a, j)]
    fillers += [project_ya] + [functools.partial(project_gates, j) for j in range(N_A_BLOCKS)]
    for g in range(GROUPS_PROJECTED_AHEAD):
        project_xbc(g)
        project_z(g)
    dense = []
    for g in range(GROUPS_PROJECTED_AHEAD, SSM_GROUPS):
        dense.append((0, g, functools.partial(project_xbc, g)))
        dense.append((0, g, functools.partial(project_z, g)))
        if fillers:
            dense.append((0, None, fillers.pop(0)))
    dense += [(0, None, f) for f in fillers]
    for g in range(0, SSM_GROUPS, YB_GROUPS):
        dense.append((g + YB_GROUPS, None, functools.partial(accumulate_yb, g, g + YB_GROUPS)))

    live, started, finished, rounds = [], 0, 0, 0
    while started < SSM_GROUPS or live:
        if started < SSM_GROUPS and rounds % GROUP_STAGGER == 0:
            for entry in [e for e in dense if e[1] == started]:
                entry[2]()
                dense.remove(entry)
            live.append(group_stages(started))
            started += 1
        still_live = [stages for stages in live if next(stages, True) is None]
        finished += len(live) - len(still_live)
        live = still_live
        for entry in [e for e in dense if e[0] <= finished][:DENSE_PIECES_PER_ROUND[rounds % 2]]:
            entry[2]()
            dense.remove(entry)
        rounds += 1
    for entry in dense:
        entry[2]()

    for j in range(N_A_BLOCKS):
        cols = slice(j * A_BLOCK, (j + 1) * A_BLOCK)
        pg = pgbuf[j]
        merged = _sigmoid(pg[:, :A_BLOCK]) * yabuf[:, cols] + _sigmoid(pg[:, A_BLOCK:]) * ybbuf[:, cols]
        mg_ref[j] = merged.astype(BF16)
    merged = jnp.concatenate([mg_ref[j] for j in range(N_A_BLOCKS)], axis=1)
    for j in range(N_A_BLOCKS):
        cols = slice(j * A_BLOCK, (j + 1) * A_BLOCK)
        o_ref[:, cols] = x_ref[:, cols] + _dot(merged, wo_ref[:, cols])


def _mixer_constants(tm):
    t = np.arange(tm)
    tril = ((t[:, None] >= t[None, :]) & (t[:, None] // CHUNK == t[None, :] // CHUNK))
    k = np.arange(LANES)[:, None]
    head = np.arange(D_INNER)[None, :] // SSM_HEAD_DIM
    expand = (k % SSM_HEADS == head) & (k < 3 * SSM_HEADS)
    r = np.arange(GROUP_WIDTH)
    bdmask = (r[:, None] // SSM_HEAD_DIM) == (r[None, :] // SSM_HEAD_DIM)
    return (jnp.asarray(tril, BF16), jnp.asarray(expand, BF16), jnp.asarray(bdmask, BF16))


def _mixer(x2d, batch, seq, gain, wmain, wdt, wg, conv_a, short_w_out, conv_b, conv_b_bias, dt_bias, a_log,
           d_skip, ssm_norm, ssm_w_out, w_out, *, tm=256):
    d = D_MODEL
    ns = seq // tm
    dtb = jnp.tile(dt_bias, HEAD_COPIES).reshape(1, LANES)
    alog = jnp.tile(a_log, HEAD_COPIES).reshape(1, LANES)
    dexp = jnp.repeat(d_skip, SSM_HEAD_DIM).reshape(1, D_INNER)
    tril, expand, bdmask = _mixer_constants(tm)
    operands = [
        gain.reshape(1, d), wmain, wdt, wg,
        conv_a, short_w_out, conv_b, conv_b_bias.reshape(1, D_XBC),
        dtb, alog, dexp,
        ssm_norm.reshape(1, D_INNER), ssm_w_out, w_out,
        tril, expand, bdmask,
    ]
    x_spec = pl.BlockSpec((tm, d), lambda b, s: (b * ns + s, 0))
    scratch = [
        pltpu.VMEM((tm, d), BF16),
        pltpu.VMEM((N_A_BLOCKS, SUBLANES + tm, A_BLOCK), F32),
        pltpu.VMEM((SSM_GROUPS, SUBLANES + tm, GROUP_XBC), F32),
        pltpu.VMEM((SSM_GROUPS, tm, GROUP_WIDTH), F32),
        pltpu.VMEM((SSM_GROUPS, D_STATE, GROUP_WIDTH), F32),
        pltpu.VMEM((3, tm // LANES, LANES, LANES), F32),
        pltpu.VMEM((N_A_BLOCKS, tm, A_BLOCK), BF16),
        pltpu.VMEM((SSM_GROUPS, tm, GROUP_WIDTH), BF16),
        pltpu.VMEM((N_A_BLOCKS, tm, A_BLOCK), BF16),
        pltpu.VMEM((N_A_BLOCKS, tm, 2 * A_BLOCK), F32),
        pltpu.VMEM((tm, d), F32),
        pltpu.VMEM((tm, d), F32),
    ]
    return pl.pallas_call(
        functools.partial(_mixer_body, tm=tm),
        grid=(batch, ns),
        in_specs=[x_spec] + [_resident(op.shape) for op in operands],
        out_specs=x_spec,
        out_shape=jax.ShapeDtypeStruct(x2d.shape, F32),
        scratch_shapes=scratch,
        compiler_params=pltpu.CompilerParams(
            dimension_semantics=("arbitrary", "arbitrary"), vmem_limit_bytes=VMEM_LIMIT_BYTES),
        name="mixer",
    )(x2d, *operands)


def kernel(x, ffn1_norm, ffn1_w_in, ffn1_w_out, mix_norm, w_in, short_conv_w, short_w_out, ssm_conv_w, ssm_conv_b, ssm_dt_bias, ssm_A_log, ssm_D, ssm_norm, ssm_w_out, w_out, ffn2_norm, ffn2_w_in, ffn2_w_out, final_norm):
    b, s, d = x.shape
    x2d = x.reshape(b * s, d)
    fin = final_norm.reshape(1, d)
    w_dt_start = W_XBC_START + D_XBC
    w_g_start = w_dt_start + SSM_HEADS
    for l in range(ffn1_norm.shape[0]):
        w_t = w_in[l].T
        wdt = jnp.tile(w_t[w_dt_start:w_g_start].T, (1, HEAD_COPIES)).astype(BF16)
        jobs = (
            _CastJob(w_t, (CAST_BLOCK, d), 0, True, count=w_dt_start),
            _CastJob(w_t, (CAST_BLOCK // 2, d), 0, True, start=w_g_start),
            _CastJob(ffn2_w_in[l], (d, CAST_BLOCK // 2), 1, False),
            _CastJob(ffn2_w_out[l], (CAST_BLOCK // 2, d), 0, False),
            _CastJob(short_w_out[l], (LANES, d), 0, False),
            _CastJob(ssm_w_out[l], (LANES, d), 0, False),
            _CastJob(w_out[l], (LANES, d), 0, False),
        )
        x2d, (wmain, wg, w2_in, w2_out, swo, sswo, wo) = _ffn(
            x2d, ffn1_norm[l].reshape(1, d), ffn1_w_in[l].astype(BF16),
            jnp.pad(ffn1_w_out[l].astype(BF16), ((0, 0), (0, PITCH_PAD))), fin,
            final_norm=False, tm=512, jobs=jobs)
        x2d = _mixer(x2d, b, s, mix_norm[l], wmain, wdt, wg, short_conv_w[l], swo,
                     ssm_conv_w[l], ssm_conv_b[l], ssm_dt_bias[l], ssm_A_log[l], ssm_D[l],
                     ssm_norm[l], sswo, wo)
        last = l == ffn1_norm.shape[0] - 1
        x2d, _ = _ffn(x2d, ffn2_norm[l].reshape(1, d), w2_in, w2_out, fin, final_norm=last, tm=1024, row_parts=2)
    return x2d.reshape(b, s, d)
```
